```python
import math
import jax, jax.numpy as jnp
from jax import lax
import numpy as np

D_MODEL = 1024
BATCH = 2
SEQ = 8192
DEPTH = 1
DEC_BATCH = 32
DEC_SEQ = 1
PAST_LEN = 8192
PAGE_SIZE = 128

H_FOX = 8
H_MOBA = 8
HEAD_DIM = 64
W_FOX = H_FOX * HEAD_DIM
W_MOBA = H_MOBA * HEAD_DIM
Q_BLOCK = 128
MOBA_BLOCK = 256
MOBA_TOPK = 3
T5_BUCKETS = 32
T5_MAX_DIST = 128
N_GROUPS = 4
EXPERTS_PER_GROUP = 8
N_EXPERTS = N_GROUPS * EXPERTS_PER_GROUP
D_EXPERT = 256
FORGET_BIAS_INIT = 3.0
EPS = 1e-6
NEG = -1e30
IN_SPLITS = (W_FOX, W_FOX, W_FOX, H_FOX, W_MOBA, W_MOBA, W_MOBA, D_MODEL, D_MODEL)
D_IN = W_FOX * 3 + H_FOX + W_MOBA * 3 + 2 * D_MODEL

kernel_name = 'hybrid_fox_moba_hmoe_step'


def _rms(x, g):
    xf = x.astype(jnp.float32)
    y = xf * lax.rsqrt(jnp.mean(xf * xf, axis=-1, keepdims=True) + EPS)
    return (y * g.astype(jnp.float32)).astype(x.dtype)


def _gather_pages(cache, layer, page_table):
    g = cache[layer, page_table]
    return g.reshape((g.shape[0], g.shape[1] * g.shape[2]) + g.shape[3:])


def _project(x, g_attn, w_in, b_forget, g_q_fox, g_k_fox, g_q_moba, g_k_moba):
    b, t, _ = x.shape
    h = _rms(x, g_attn)
    z = jnp.einsum('btd,de->bte', h, w_in)
    points = [int(p) for p in np.cumsum(IN_SPLITS)[:-1]]
    qa, ka, va, fa, qb, kb, vb, ga, gb = jnp.split(z, points, axis=-1)
    qa = _rms(qa.reshape(b, t, H_FOX, HEAD_DIM), g_q_fox)
    ka = _rms(ka.reshape(b, t, H_FOX, HEAD_DIM), g_k_fox)
    va = va.reshape(b, t, H_FOX, HEAD_DIM)
    logf = jax.nn.log_sigmoid(fa.astype(jnp.float32) + b_forget.astype(jnp.float32))
    qb = _rms(qb.reshape(b, t, H_MOBA, HEAD_DIM), g_q_moba)
    kb = _rms(kb.reshape(b, t, H_MOBA, HEAD_DIM), g_k_moba)
    vb = vb.reshape(b, t, H_MOBA, HEAD_DIM)
    return qa, ka, va, logf, qb, kb, vb, ga, gb


def _fox_attend(q, k, v, cq, ck, qpos, kpos):
    b, nq, h, dh = q.shape
    logits = jnp.einsum('bqhd,bkhd->bhqk', q, k, preferred_element_type=jnp.float32) * (dh ** -0.5)
    logits = logits + cq.transpose(0, 2, 1)[..., None] - ck.transpose(0, 2, 1)[:, :, None, :]
    logits = jnp.where(kpos[None, :] <= qpos[:, None], logits, NEG)
    p = jax.nn.softmax(logits, axis=-1)
    o = jnp.einsum('bhqk,bkhd->bqhd', p.astype(v.dtype), v)
    return o.reshape(b, nq, h * dh)


def _fox_prompt(q, k, v, logf):
    b, s, h, dh = q.shape
    cum = jnp.cumsum(logf, axis=1)
    kpos = jnp.arange(s)

    def block(i):
        s0 = i * Q_BLOCK
        qc = lax.dynamic_slice_in_dim(q, s0, Q_BLOCK, axis=1)
        cq = lax.dynamic_slice_in_dim(cum, s0, Q_BLOCK, axis=1)
        return _fox_attend(qc, k, v, cq, cum, s0 + jnp.arange(Q_BLOCK), kpos)

    out = lax.map(block, jnp.arange(s // Q_BLOCK))
    return out.transpose(1, 0, 2, 3).reshape(b, s, h * dh)


def _t5_bucket(dist):
    max_exact = T5_BUCKETS // 2
    n = jnp.maximum(dist, 0)
    nf = jnp.maximum(n, max_exact).astype(jnp.float32)
    large = max_exact + (jnp.log(nf / max_exact) / math.log(T5_MAX_DIST / max_exact)
                         * (T5_BUCKETS - max_exact)).astype(jnp.int32)
    return jnp.where(n < max_exact, n, jnp.minimum(large, T5_BUCKETS - 1))


def _moba_blocks(k, v):
    b, length, h, dh = k.shape
    nb = -(-length // MOBA_BLOCK)
    pad = ((0, 0), (0, nb * MOBA_BLOCK - length), (0, 0), (0, 0))
    kb = jnp.pad(k, pad).reshape(b, nb, MOBA_BLOCK, h, dh).transpose(0, 3, 1, 2, 4)
    vb = jnp.pad(v, pad).reshape(b, nb, MOBA_BLOCK, h, dh).transpose(0, 3, 1, 2, 4)
    kmean = jnp.mean(kb.astype(jnp.float32), axis=3).astype(k.dtype)
    return kb, vb, kmean


def _moba_attend(q, qpos, own_idx, kb, vb, kmean, t5_table):
    b, h, g, nq, dh = q.shape
    nb = kb.shape[2]
    scale = dh ** -0.5
    tab = t5_table.T
    bi = jnp.arange(b)[:, None, None, None, None]
    hi = jnp.arange(h)[None, :, None, None, None]
    rel = jnp.arange(MOBA_BLOCK)
    gate = jnp.einsum('bhgqd,bhnd->bhgqn', q, kmean, preferred_element_type=jnp.float32)
    gate = jnp.where(jnp.arange(nb) < own_idx[:, None, None], gate, NEG)
    sel = lax.top_k(gate, min(MOBA_TOPK, nb))[1]
    valid = sel < own_idx[:, None, None]
    ksel = kb[bi, hi, sel]
    vsel = vb[bi, hi, sel]
    sel_dist = qpos[:, :, None, None] - (sel[..., None] * MOBA_BLOCK + rel)
    sel_logits = jnp.einsum('bhgqd,bhgqjrd->bhgqjr', q, ksel, preferred_element_type=jnp.float32) * scale
    sel_logits = sel_logits + tab[hi[..., None], _t5_bucket(sel_dist)]
    sel_logits = jnp.where(valid[..., None], sel_logits, NEG).reshape(b, h, g, nq, -1)
    kown = kb[:, :, own_idx]
    vown = vb[:, :, own_idx]
    own_dist = qpos[:, :, None] - (own_idx[:, None] * MOBA_BLOCK + rel)[:, None, :]
    own_logits = jnp.einsum('bhgqd,bhgrd->bhgqr', q, kown, preferred_element_type=jnp.float32) * scale
    own_logits = own_logits + tab[:, _t5_bucket(own_dist)]
    own_logits = jnp.where(own_dist >= 0, own_logits, NEG)
    nsel = sel_logits.shape[-1]
    p = jax.nn.softmax(jnp.concatenate([sel_logits, own_logits], axis=-1), axis=-1).astype(vb.dtype)
    out = jnp.einsum('bhgqk,bhgqkd->bhgqd', p[..., :nsel], vsel.reshape(b, h, g, nq, nsel, dh))
    return out + jnp.einsum('bhgqr,bhgrd->bhgqd', p[..., nsel:], vown)


def _moba_prompt(q, k, v, t5_table):
    b, s, h, dh = q.shape
    kb, vb, kmean = _moba_blocks(k, v)
    qh = q.transpose(0, 2, 1, 3)

    def block(c):
        s0 = c * Q_BLOCK
        qc = lax.dynamic_slice_in_dim(qh, s0, Q_BLOCK, axis=2)[:, :, None]
        qpos = (s0 + jnp.arange(Q_BLOCK))[None]
        own = (s0 // MOBA_BLOCK)[None]
        return _moba_attend(qc, qpos, own, kb, vb, kmean, t5_table)[:, :, 0]

    out = lax.map(block, jnp.arange(s // Q_BLOCK))
    return out.transpose(1, 0, 3, 2, 4).reshape(b, s, h * dh)


def _merge(oa, ob, ga, gb, w_branch_fox, w_branch_moba, w_out):
    m = jax.nn.sigmoid(ga) * (oa @ w_branch_fox) + jax.nn.sigmoid(gb) * (ob @ w_branch_moba)
    return m @ w_out


def _hmoe(x, g_ffn, w_router_group, w_router_expert, w_e_gate, w_e_up, w_e_down):
    b, t, d = x.shape
    h = _rms(x, g_ffn).reshape(b * t, d)
    glog = (h @ w_router_group).astype(jnp.float32)
    gprob = jax.nn.softmax(glog, axis=-1)
    gsel = jnp.argmax(glog, axis=-1)
    elog = jnp.einsum('nd,dge->nge', h, w_router_expert).astype(jnp.float32)
    elog = jnp.take_along_axis(elog, gsel[:, None, None], axis=1)[:, 0]
    ev, ei = lax.top_k(elog, 2)
    ew = jax.nn.softmax(ev, axis=-1) * jnp.take_along_axis(gprob, gsel[:, None], axis=1)
    eid = gsel[:, None] * EXPERTS_PER_GROUP + ei
    gates = jnp.sum(jax.nn.one_hot(eid, N_EXPERTS, dtype=jnp.float32) * ew[..., None], axis=1)

    def step(acc, inp):
        wg, wu, wd, gt = inp
        a = jax.nn.silu(h @ wg) * (h @ wu)
        return acc + (a @ wd) * gt[:, None].astype(h.dtype), None

    y, _ = lax.scan(step, jnp.zeros_like(h), (w_e_gate, w_e_up, w_e_down, gates.T))
    return y.reshape(b, t, d)


def setup_inputs(seed: int = 0) -> dict:
    key = jax.random.key(seed)
    ks = jax.random.split(key, 32)
    f32 = jnp.float32
    n_pages = PAST_LEN // PAGE_SIZE
    n_phys = (DEC_BATCH * n_pages * 5) // 4

    def nrm(k, shape, scale):
        return jax.random.normal(k, shape, f32) * scale

    kv_fox = (DEPTH, n_phys, PAGE_SIZE, H_FOX, HEAD_DIM)
    kv_moba = (DEPTH, n_phys, PAGE_SIZE, H_MOBA, HEAD_DIM)
    page_table = jax.random.permutation(ks[7], n_phys)[:DEC_BATCH * n_pages]
    page_table = page_table.reshape(DEC_BATCH, n_pages).astype(jnp.int32)
    return {
        'x_prompt': nrm(ks[0], (BATCH, SEQ, D_MODEL), 1.0),
        'x_sample': nrm(ks[1], (DEC_BATCH, DEC_SEQ, D_MODEL), 1.0),
        'cache_fox_k': nrm(ks[2], kv_fox, 1.0),
        'cache_fox_v': nrm(ks[3], kv_fox, 1.0),
        'cache_fox_logf': jax.nn.log_sigmoid(FORGET_BIAS_INIT + nrm(ks[4], (DEPTH, n_phys, PAGE_SIZE, H_FOX), 1.0)),
        'cache_moba_k': nrm(ks[5], kv_moba, 1.0),
        'cache_moba_v': nrm(ks[6], kv_moba, 1.0),
        'page_table': page_table,
        'g_attn': 1.0 + nrm(ks[8], (DEPTH, D_MODEL), 0.02),
        'w_in': nrm(ks[9], (DEPTH, D_MODEL, D_IN), D_MODEL ** -0.5),
        'b_forget': FORGET_BIAS_INIT + nrm(ks[10], (DEPTH, H_FOX), 0.1),
        'g_q_fox': 1.0 + nrm(ks[11], (DEPTH, HEAD_DIM), 0.02),
        'g_k_fox': 1.0 + nrm(ks[12], (DEPTH, HEAD_DIM), 0.02),
        'g_q_moba': 1.0 + nrm(ks[13], (DEPTH, HEAD_DIM), 0.02),
        'g_k_moba': 1.0 + nrm(ks[14], (DEPTH, HEAD_DIM), 0.02),
        't5_table': nrm(ks[15], (T5_BUCKETS, H_MOBA), 0.5),
        'w_branch_fox': nrm(ks[16], (DEPTH, W_FOX, D_MODEL), W_FOX ** -0.5),
        'w_branch_moba': nrm(ks[17], (DEPTH, W_MOBA, D_MODEL), W_MOBA ** -0.5),
        'w_out': nrm(ks[18], (DEPTH, D_MODEL, D_MODEL), D_MODEL ** -0.5),
        'g_ffn': 1.0 + nrm(ks[19], (DEPTH, D_MODEL), 0.02),
        'w_router_group': nrm(ks[20], (DEPTH, D_MODEL, N_GROUPS), D_MODEL ** -0.5),
        'w_router_expert': nrm(ks[21], (DEPTH, D_MODEL, N_GROUPS, EXPERTS_PER_GROUP), D_MODEL ** -0.5),
        'w_e_gate': nrm(ks[22], (DEPTH, N_EXPERTS, D_MODEL, D_EXPERT), D_MODEL ** -0.5),
        'w_e_up': nrm(ks[23], (DEPTH, N_EXPERTS, D_MODEL, D_EXPERT), D_MODEL ** -0.5),
        'w_e_down': nrm(ks[24], (DEPTH, N_EXPERTS, D_EXPERT, D_MODEL), D_EXPERT ** -0.5),
    }


def reference(x_prompt, x_sample, cache_fox_k, cache_fox_v, cache_fox_logf, cache_moba_k, cache_moba_v,
              page_table, g_attn, w_in, b_forget, g_q_fox, g_k_fox, g_q_moba, g_k_moba, t5_table,
              w_branch_fox, w_branch_moba, w_out, g_ffn, w_router_group, w_router_expert,
              w_e_gate, w_e_up, w_e_down):
    f32 = jnp.float32
    x_p, x_s = x_prompt, x_sample
    db, nt, _ = x_s.shape
    p_len = page_table.shape[1] * cache_fox_k.shape[2]
    qpos_s = p_len + jnp.arange(nt)
    fk_p, fv_p, fl_p, mk_p, mv_p = [], [], [], [], []
    fk_s, fv_s, fl_s, mk_s, mv_s = [], [], [], [], []
    for l in range(DEPTH):
        proj = (g_attn[l], w_in[l], b_forget[l], g_q_fox[l], g_k_fox[l], g_q_moba[l], g_k_moba[l])
        moe = (g_ffn[l], w_router_group[l], w_router_expert[l], w_e_gate[l], w_e_up[l], w_e_down[l])
        qa, ka, va, la, qb, kb, vb, ga, gb = _project(x_p, *proj)
        oa = _fox_prompt(qa, ka, va, la)
        ob = _moba_prompt(qb, kb, vb, t5_table)
        x_p = x_p + _merge(oa, ob, ga, gb, w_branch_fox[l], w_branch_moba[l], w_out[l])
        x_p = x_p + _hmoe(x_p, *moe)
        fk_p.append(ka); fv_p.append(va); fl_p.append(la); mk_p.append(kb); mv_p.append(vb)
        qa_s, ka_s, va_s, la_s, qb_s, kb_s, vb_s, ga_s, gb_s = _project(x_s, *proj)
        ka_all = jnp.concatenate([_gather_pages(cache_fox_k, l, page_table), ka_s], axis=1)
        va_all = jnp.concatenate([_gather_pages(cache_fox_v, l, page_table), va_s], axis=1)
        la_all = jnp.concatenate([_gather_pages(cache_fox_logf, l, page_table).astype(f32), la_s], axis=1)
        cum = jnp.cumsum(la_all, axis=1)
        oa_s = _fox_attend(qa_s, ka_all, va_all, cum[:, p_len:], cum, qpos_s, jnp.arange(ka_all.shape[1]))
        kb_all = jnp.concatenate([_gather_pages(cache_moba_k, l, page_table), kb_s], axis=1)
        vb_all = jnp.concatenate([_gather_pages(cache_moba_v, l, page_table), vb_s], axis=1)
        blk_k, blk_v, kmean = _moba_blocks(kb_all, vb_all)
        qh = qb_s.transpose(0, 2, 1, 3)[:, :, :, None]
        ob_s = _moba_attend(qh, qpos_s[:, None], qpos_s // MOBA_BLOCK, blk_k, blk_v, kmean, t5_table)[:, :, :, 0]
        ob_s = ob_s.transpose(0, 2, 1, 3).reshape(db, nt, W_MOBA)
        x_s = x_s + _merge(oa_s, ob_s, ga_s, gb_s, w_branch_fox[l], w_branch_moba[l], w_out[l])
        x_s = x_s + _hmoe(x_s, *moe)
        fk_s.append(ka_s); fv_s.append(va_s); fl_s.append(la_s); mk_s.append(kb_s); mv_s.append(vb_s)
    y_prompt, y_sample = x_p, x_s
    fox_k_prompt, fox_v_prompt, fox_logf_prompt = jnp.stack(fk_p), jnp.stack(fv_p), jnp.stack(fl_p)
    moba_k_prompt, moba_v_prompt = jnp.stack(mk_p), jnp.stack(mv_p)
    fox_k_sample, fox_v_sample, fox_logf_sample = jnp.stack(fk_s), jnp.stack(fv_s), jnp.stack(fl_s)
    moba_k_sample, moba_v_sample = jnp.stack(mk_s), jnp.stack(mv_s)
    return (y_prompt, y_sample, fox_k_prompt, fox_v_prompt, fox_logf_prompt, moba_k_prompt, moba_v_prompt,
            fox_k_sample, fox_v_sample, fox_logf_sample, moba_k_sample, moba_v_sample)
```

```python
import functools
import math

import numpy as np
import jax
import jax.numpy as jnp
from jax import lax
from jax.experimental import pallas as pl
from jax.experimental.pallas import tpu as pltpu

F32 = jnp.float32
BF16 = jnp.bfloat16
HIGHEST = lax.Precision.HIGHEST

D_MODEL = 1024
N_HEADS = 8
HEAD_DIM = 64
W_HEADS = N_HEADS * HEAD_DIM
MOBA_BLOCK = 256
MOBA_TOPK = 3
T5_BUCKETS = 32
T5_MAX_DIST = 128
N_GROUPS = 4
EXPERTS_PER_GROUP = 8
N_EXPERTS = N_GROUPS * EXPERTS_PER_GROUP
D_EXPERT = 256
EPS = 1e-6
NEG = -1e30
SCALE = HEAD_DIM ** -0.5
LANES = 128
VMEM_LIMIT = 56 * 1024 * 1024


def _t5_thresholds():
    max_exact = T5_BUCKETS // 2
    n = np.arange(0, 4 * T5_MAX_DIST)
    nf = np.maximum(n, max_exact).astype(np.float32)
    large = max_exact + (np.log(nf / np.float32(max_exact)) / np.float32(math.log(T5_MAX_DIST / max_exact))
                         * np.float32(T5_BUCKETS - max_exact)).astype(np.int32)
    bucket = np.where(n < max_exact, n, np.minimum(large, T5_BUCKETS - 1))
    return [int(np.argmax(bucket >= j)) for j in range(1, T5_BUCKETS)]


T5_THRESHOLDS = _t5_thresholds()


def _params(*sem):
    return pltpu.CompilerParams(dimension_semantics=sem, vmem_limit_bytes=VMEM_LIMIT)


def _proj_kernel(x_ref, g_ref, w_ref, wf_ref, bf_ref, gains_ref, seg_ref,
                 qa_ref, ka_ref, va_ref, lf_ref, qb_ref, kb_ref, vb_ref, sga_ref, sgb_ref, km_ref):
    x = x_ref[...]
    h = x * lax.rsqrt(jnp.mean(x * x, axis=-1, keepdims=True) + EPS) * g_ref[...]
    hb = h.astype(BF16)
    seg = seg_ref[...]

    def proj(i, width=W_HEADS):
        return jnp.dot(hb, w_ref[:, i:i + width], preferred_element_type=F32)

    def headnorm(z, row):
        ms = jnp.dot((z * z).astype(BF16), seg, preferred_element_type=F32)
        return z * lax.rsqrt(ms + EPS) * gains_ref[row:row + 1, :]

    qa_ref[...] = headnorm(proj(0), 0)
    ka_ref[...] = headnorm(proj(W_HEADS), 1)
    va_ref[...] = proj(2 * W_HEADS)
    qb_ref[...] = headnorm(proj(3 * W_HEADS), 2)
    kb = headnorm(proj(4 * W_HEADS), 3)
    kb_ref[...] = kb
    km_ref[...] = jnp.mean(kb, axis=0, keepdims=True)
    vb_ref[...] = proj(5 * W_HEADS)
    sga_ref[...] = jax.nn.sigmoid(proj(6 * W_HEADS, D_MODEL)).astype(BF16)
    sgb_ref[...] = jax.nn.sigmoid(proj(6 * W_HEADS + D_MODEL, D_MODEL)).astype(BF16)
    f = jnp.dot(hb, wf_ref[...], preferred_element_type=F32) + bf_ref[...]
    lf_ref[...] = jnp.minimum(f, 0.0) - jnp.log1p(jnp.exp(-jnp.abs(f)))


def _project(x2d, tm, g_attn, w_main, w_f, b_f, gains, seg):
    n = x2d.shape[0]
    row = lambda i: (i, 0)
    const = lambda i: (0, 0)
    wide = lambda w, dt: jax.ShapeDtypeStruct((n, w), dt)
    out_shape = (wide(W_HEADS, F32), wide(W_HEADS, F32), wide(W_HEADS, F32), wide(LANES, F32),
                 wide(W_HEADS, F32), wide(W_HEADS, F32), wide(W_HEADS, F32),
                 wide(D_MODEL, BF16), wide(D_MODEL, BF16),
                 jax.ShapeDtypeStruct((n // tm, 1, W_HEADS), F32))
    blk = lambda w: pl.BlockSpec((tm, w), row)
    out_specs = (blk(W_HEADS), blk(W_HEADS), blk(W_HEADS), blk(LANES), blk(W_HEADS), blk(W_HEADS), blk(W_HEADS),
                 blk(D_MODEL), blk(D_MODEL), pl.BlockSpec((None, 1, W_HEADS), lambda i: (i, 0, 0)))
    in_specs = [blk(D_MODEL), pl.BlockSpec((1, D_MODEL), const), pl.BlockSpec(w_main.shape, const),
                pl.BlockSpec(w_f.shape, const), pl.BlockSpec((1, LANES), const),
                pl.BlockSpec(gains.shape, const), pl.BlockSpec(seg.shape, const)]
    return pl.pallas_call(_proj_kernel, grid=(n // tm,), in_specs=in_specs, out_specs=out_specs,
                          out_shape=out_shape, compiler_params=_params("arbitrary"), name="proj")(
        x2d, g_attn, w_main, w_f, b_f, gains, seg)


def _cumsum_kernel(lf_ref, o_ref, carry_ref, *, tc):
    @pl.when(pl.program_id(1) == 0)
    def _():
        carry_ref[...] = jnp.zeros_like(carry_ref)

    r = lax.broadcasted_iota(jnp.int32, (tc, tc), 0)
    c = lax.broadcasted_iota(jnp.int32, (tc, tc), 1)
    tri = jnp.where(c <= r, 1.0, 0.0).astype(F32)
    out = jnp.dot(tri, lf_ref[...], preferred_element_type=F32, precision=HIGHEST) + carry_ref[...]
    o_ref[...] = out
    carry_ref[...] = out[tc - 1:tc, :]


def _cumsum(lf, batch, seq, tc=256):
    nchunk = seq // tc
    spec = pl.BlockSpec((tc, LANES), lambda b, i: (b * nchunk + i, 0))
    return pl.pallas_call(functools.partial(_cumsum_kernel, tc=tc), grid=(batch, nchunk),
                          in_specs=[spec], out_specs=spec,
                          out_shape=jax.ShapeDtypeStruct(lf.shape, F32),
                          scratch_shapes=[pltpu.VMEM((1, LANES), F32)],
                          compiler_params=_params("arbitrary", "arbitrary"), name="logf_cumsum")(lf)


def _online_update(carry, s, v):
    m, l, acc = carry
    m_new = jnp.maximum(m, jnp.max(s, axis=-1, keepdims=True))
    alpha = jnp.exp(m - m_new)
    p = jnp.exp(s - m_new)
    l = alpha * l + jnp.sum(p, axis=-1, keepdims=True)
    acc = alpha * acc + jnp.dot(p.astype(BF16), v, preferred_element_type=F32)
    return m_new, l, acc


def _fox_kernel(q_ref, kt_ref, v_ref, cq_ref, ck_ref, o_ref, *, tq):
    qi = pl.program_id(2)
    q = q_ref[...]
    cq = cq_ref[...]

    def logits(c):
        k0 = pl.multiple_of(c * tq, tq)
        s = jnp.dot(q, kt_ref[:, pl.ds(k0, tq)], preferred_element_type=F32)
        return s + cq - ck_ref[:, pl.ds(k0, tq)], v_ref[pl.ds(k0, tq), :]

    def body(c, carry):
        s, v = logits(c)
        return _online_update(carry, s, v)

    init = (jnp.full((tq, 1), NEG, F32), jnp.zeros((tq, 1), F32), jnp.zeros((tq, HEAD_DIM), F32))
    carry = lax.fori_loop(0, qi, body, init)
    s, v = logits(qi)
    row = lax.broadcasted_iota(jnp.int32, (tq, tq), 0)
    col = lax.broadcasted_iota(jnp.int32, (tq, tq), 1)
    m, l, acc = _online_update(carry, jnp.where(col <= row, s, NEG), v)
    o_ref[...] = (acc / l).astype(o_ref.dtype)


def _fox_prompt(q, kt, v, cq, ck, tq=512):
    b, h, s, _ = q.shape
    grid = (b, h, s // tq)
    in_specs = [pl.BlockSpec((None, None, tq, HEAD_DIM), lambda b, h, i: (b, h, i, 0)),
                pl.BlockSpec((None, None, HEAD_DIM, s), lambda b, h, i: (b, h, 0, 0)),
                pl.BlockSpec((None, None, s, HEAD_DIM), lambda b, h, i: (b, h, 0, 0)),
                pl.BlockSpec((None, None, tq, 1), lambda b, h, i: (b, h, i, 0)),
                pl.BlockSpec((None, None, 1, s), lambda b, h, i: (b, h, 0, 0))]
    out_specs = pl.BlockSpec((None, None, tq, HEAD_DIM), lambda b, h, i: (b, h, i, 0))
    return pl.pallas_call(functools.partial(_fox_kernel, tq=tq), grid=grid, in_specs=in_specs, out_specs=out_specs,
                          out_shape=jax.ShapeDtypeStruct((b, h, s, HEAD_DIM), BF16),
                          compiler_params=_params("arbitrary", "arbitrary", "arbitrary"), name="fox_prompt")(
        q, kt, v, cq, ck)


def _t5_select(dist, table_at):
    bias = table_at(0)
    for j, thr in enumerate(T5_THRESHOLDS, start=1):
        bias = jnp.where(dist >= thr, table_at(j), bias)
    return bias


def _bias_kernel(tab_ref, o_ref):
    hh = pl.program_id(0)
    i = lax.broadcasted_iota(jnp.int32, (MOBA_BLOCK, 2 * MOBA_BLOCK), 0)
    j = lax.broadcasted_iota(jnp.int32, (MOBA_BLOCK, 2 * MOBA_BLOCK), 1)
    dist = MOBA_BLOCK + i - j
    o_ref[...] = _t5_select(dist, lambda bkt: tab_ref[bkt, hh])


def _moba_bias(t5_table):
    return pl.pallas_call(_bias_kernel, grid=(N_HEADS,),
                          in_specs=[pl.BlockSpec(memory_space=pltpu.SMEM)],
                          out_specs=pl.BlockSpec((None, MOBA_BLOCK, 2 * MOBA_BLOCK), lambda h: (h, 0, 0)),
                          out_shape=jax.ShapeDtypeStruct((N_HEADS, MOBA_BLOCK, 2 * MOBA_BLOCK), F32),
                          compiler_params=_params("arbitrary"), name="moba_bias")(t5_table)


def _top_blocks(gate, n_eligible):
    lane = lax.broadcasted_iota(jnp.int32, gate.shape, 1)
    g = jnp.where(lane < n_eligible, gate, -jnp.inf)
    sel = jnp.zeros(gate.shape, F32)
    for _ in range(MOBA_TOPK):
        mx = jnp.max(g, axis=-1, keepdims=True)
        idx = jnp.min(jnp.where(g == mx, lane, gate.shape[1]), axis=-1, keepdims=True)
        pick = jnp.logical_and(lane == idx, mx > -jnp.inf)
        sel = jnp.where(pick, 1.0, sel)
        g = jnp.where(pick, -jnp.inf, g)
    return sel


def _moba_kernel(q_ref, q32_ref, kt_ref, v_ref, kmt_ref, bias_ref, o_ref, *, nb):
    blk = MOBA_BLOCK
    qi = pl.program_id(2)
    q = q_ref[...]
    gate = jnp.dot(q32_ref[...], kmt_ref[...], preferred_element_type=F32, precision=HIGHEST)
    sel = _top_blocks(gate, qi).astype(BF16)
    far_bias = bias_ref[blk - 1:blk, 0:1]

    k0 = pl.multiple_of(qi * blk, blk)
    s = jnp.dot(q, kt_ref[:, pl.ds(k0, blk)], preferred_element_type=F32) + bias_ref[:, blk:]
    row = lax.broadcasted_iota(jnp.int32, (blk, blk), 0)
    col = lax.broadcasted_iota(jnp.int32, (blk, blk), 1)
    init = (jnp.full((blk, 1), NEG, F32), jnp.zeros((blk, 1), F32), jnp.zeros((blk, HEAD_DIM), F32))
    carry = _online_update(init, jnp.where(col <= row, s, NEG), v_ref[pl.ds(k0, blk), :])

    def body(n, carry):
        k0 = pl.multiple_of(n * blk, blk)
        s = jnp.dot(q, kt_ref[:, pl.ds(k0, blk)], preferred_element_type=F32)
        s = s + jnp.where(n == qi - 1, bias_ref[:, :blk], far_bias)
        onehot = (lax.broadcasted_iota(jnp.int32, (nb, blk), 0) == n).astype(BF16)
        chosen = jnp.dot(sel, onehot, preferred_element_type=F32)
        return _online_update(carry, jnp.where(chosen > 0.5, s, NEG), v_ref[pl.ds(k0, blk), :])

    m, l, acc = lax.fori_loop(0, qi, body, carry)
    o_ref[...] = (acc / l).astype(o_ref.dtype)


def _moba_prompt(q, q32, kt, v, kmt, bias):
    b, h, s, _ = q.shape
    blk = MOBA_BLOCK
    nb = s // blk
    qspec = pl.BlockSpec((None, None, blk, HEAD_DIM), lambda b, h, i: (b, h, i, 0))
    in_specs = [qspec, qspec,
                pl.BlockSpec((None, None, HEAD_DIM, s), lambda b, h, i: (b, h, 0, 0)),
                pl.BlockSpec((None, None, s, HEAD_DIM), lambda b, h, i: (b, h, 0, 0)),
                pl.BlockSpec((None, None, HEAD_DIM, nb), lambda b, h, i: (b, h, 0, 0)),
                pl.BlockSpec((None, blk, 2 * blk), lambda b, h, i: (h, 0, 0))]
    return pl.pallas_call(functools.partial(_moba_kernel, nb=nb), grid=(b, h, nb), in_specs=in_specs,
                          out_specs=qspec, out_shape=jax.ShapeDtypeStruct((b, h, s, HEAD_DIM), BF16),
                          compiler_params=_params("arbitrary", "arbitrary", "arbitrary"), name="moba_prompt")(
        q, q32, kt, v, kmt, bias)


def _merge_kernel(x_ref, oa_ref, ob_ref, sga_ref, sgb_ref, wbf_ref, wbm_ref, wout_ref, g_ref, wr_ref,
                  x1_ref, h2_ref, route_ref):
    ma = jnp.dot(oa_ref[...], wbf_ref[...], preferred_element_type=F32)
    mb = jnp.dot(ob_ref[...], wbm_ref[...], preferred_element_type=F32)
    mix = sga_ref[...].astype(F32) * ma + sgb_ref[...].astype(F32) * mb
    x1 = x_ref[...] + jnp.dot(mix.astype(BF16), wout_ref[...], preferred_element_type=F32)
    x1_ref[...] = x1
    h2 = x1 * lax.rsqrt(jnp.mean(x1 * x1, axis=-1, keepdims=True) + EPS) * g_ref[...]
    h2_ref[...] = h2.astype(BF16)

    logits = jnp.dot(h2, wr_ref[...], preferred_element_type=F32, precision=HIGHEST)
    lane = lax.broadcasted_iota(jnp.int32, logits.shape, 1)
    big = LANES

    def argmax_low(vals):
        mx = jnp.max(vals, axis=-1, keepdims=True)
        return mx, jnp.min(jnp.where(vals == mx, lane, big), axis=-1, keepdims=True)

    glog = jnp.where(lane < N_GROUPS, logits, -jnp.inf)
    gmax, gsel = argmax_low(glog)
    gprob = 1.0 / jnp.sum(jnp.exp(glog - gmax), axis=-1, keepdims=True)
    first = N_GROUPS + gsel * EXPERTS_PER_GROUP
    in_group = jnp.logical_and(lane >= first, lane < first + EXPERTS_PER_GROUP)
    elog = jnp.where(in_group, logits, -jnp.inf)
    v0, i0 = argmax_low(elog)
    v1, i1 = argmax_low(jnp.where(lane == i0, -jnp.inf, elog))
    e1 = jnp.exp(v1 - v0)
    w0 = gprob / (1.0 + e1)
    w1 = gprob * e1 / (1.0 + e1)
    route = jnp.where(lane == 0, (i0 - N_GROUPS).astype(F32),
                      jnp.where(lane == 1, (i1 - N_GROUPS).astype(F32),
                                jnp.where(lane == 2, w0, jnp.where(lane == 3, w1, 0.0))))
    route_ref[...] = route


def _merge(x2d, oa, ob, sga, sgb, wbf, wbm, wout, g_ffn, w_router, tm):
    n = x2d.shape[0]
    row = lambda i: (i, 0)
    const = lambda i: (0, 0)
    blk = lambda w: pl.BlockSpec((tm, w), row)
    full = lambda a: pl.BlockSpec(a.shape, const)
    in_specs = [blk(D_MODEL), blk(W_HEADS), blk(W_HEADS), blk(D_MODEL), blk(D_MODEL),
                full(wbf), full(wbm), full(wout), full(g_ffn), full(w_router)]
    out_shape = (jax.ShapeDtypeStruct((n, D_MODEL), F32), jax.ShapeDtypeStruct((n, D_MODEL), BF16),
                 jax.ShapeDtypeStruct((n, LANES), F32))
    return pl.pallas_call(_merge_kernel, grid=(n // tm,), in_specs=in_specs,
                          out_specs=(blk(D_MODEL), blk(D_MODEL), blk(LANES)), out_shape=out_shape,
                          compiler_params=_params("arbitrary"), name="merge_router")(
        x2d, oa, ob, sga, sgb, wbf, wbm, wout, g_ffn, w_router)


def _moe_kernel(x1_ref, h2_ref, route_ref, wg_ref, wu_ref, wd_ref, o_ref):
    e = pl.program_id(1)

    @pl.when(e == 0)
    def _():
        o_ref[...] = x1_ref[...]

    h = h2_ref[...]
    a = jax.nn.silu(jnp.dot(h, wg_ref[...], preferred_element_type=F32)) * \
        jnp.dot(h, wu_ref[...], preferred_element_type=F32)
    y = jnp.dot(a.astype(BF16), wd_ref[...], preferred_element_type=F32)
    r = route_ref[...]
    ef = e.astype(F32)
    gate = jnp.where(r[:, 0:1] == ef, r[:, 2:3], 0.0) + jnp.where(r[:, 1:2] == ef, r[:, 3:4], 0.0)
    o_ref[...] += y * gate


def _moe(x1, h2, route, wg, wu, wd, tm):
    n = x1.shape[0]
    row = lambda i, e: (i, 0)
    in_specs = [pl.BlockSpec((tm, D_MODEL), row), pl.BlockSpec((tm, D_MODEL), row), pl.BlockSpec((tm, LANES), row),
                pl.BlockSpec((None, D_MODEL, D_EXPERT), lambda i, e: (e, 0, 0)),
                pl.BlockSpec((None, D_MODEL, D_EXPERT), lambda i, e: (e, 0, 0)),
                pl.BlockSpec((None, D_EXPERT, D_MODEL), lambda i, e: (e, 0, 0))]
    return pl.pallas_call(_moe_kernel, grid=(n // tm, N_EXPERTS), in_specs=in_specs,
                          out_specs=pl.BlockSpec((tm, D_MODEL), row),
                          out_shape=jax.ShapeDtypeStruct((n, D_MODEL), F32),
                          compiler_params=_params("arbitrary", "arbitrary"), name="experts")(
        x1, h2, route, wg, wu, wd)


def _decode_update(m_ref, l_ref, acc_ref, s, v):
    m = m_ref[...]
    m_new = jnp.maximum(m, jnp.max(s, axis=0))
    alpha = jnp.exp(m - m_new)
    p = jnp.exp(s - m_new[None])
    l_ref[...] = alpha * l_ref[...] + jnp.sum(p, axis=0)
    acc_ref[...] = alpha * acc_ref[...] + jnp.sum(p * v, axis=0)
    m_ref[...] = m_new


def _head_diag():
    hh = lax.broadcasted_iota(jnp.int32, (N_HEADS, HEAD_DIM), 0)
    ll = lax.broadcasted_iota(jnp.int32, (N_HEADS, HEAD_DIM), 1)
    return jnp.where(hh == ll, 1.0, 0.0).astype(F32)


def _fox_decode_kernel(pt_ref, q_ref, kn_ref, vn_ref, lfn_ref, k_ref, v_ref, lf_ref, o_ref,
                       m_ref, l_ref, acc_ref, carry_ref, *, page):
    j = pl.program_id(1)
    q = q_ref[...] * SCALE
    diag = _head_diag()

    @pl.when(j == 0)
    def _():
        m_ref[...] = jnp.sum(q * kn_ref[...], axis=-1, keepdims=True)
        l_ref[...] = jnp.ones_like(l_ref)
        acc_ref[...] = vn_ref[...]
        carry_ref[...] = lfn_ref[...]

    lf = lf_ref[...]
    spread = jnp.where(lax.broadcasted_iota(jnp.int32, (N_HEADS, HEAD_DIM), 0)
                       == lax.broadcasted_iota(jnp.int32, (N_HEADS, HEAD_DIM), 1), 1.0, 0.0).astype(F32)
    lf_wide = jnp.dot(lf, spread, preferred_element_type=F32, precision=HIGHEST)
    rr = lax.broadcasted_iota(jnp.int32, (page * N_HEADS, page), 0) // N_HEADS
    cc = lax.broadcasted_iota(jnp.int32, (page * N_HEADS, page), 1)
    later = jnp.where(cc > rr, 1.0, 0.0).astype(F32)
    decay = jnp.dot(later, lf_wide, preferred_element_type=F32, precision=HIGHEST)
    decay = decay.reshape(page, N_HEADS, HEAD_DIM) * diag[None]
    total = jnp.sum(jnp.dot(jnp.ones((N_HEADS, page), F32), lf_wide, preferred_element_type=F32,
                            precision=HIGHEST) * diag, axis=-1, keepdims=True)

    carry = carry_ref[...]
    s = jnp.sum(k_ref[...] * q[None] + decay, axis=-1, keepdims=True) + carry[None]
    _decode_update(m_ref, l_ref, acc_ref, s, v_ref[...])
    carry_ref[...] = carry + total

    @pl.when(j == pl.num_programs(1) - 1)
    def _():
        o_ref[...] = acc_ref[...] / l_ref[...]


def _fox_decode(page_table, q, kn, vn, lfn, cache_k, cache_v, cache_lf):
    nb, n_pages = page_table.shape
    page = cache_k.shape[2]
    tok = lambda b, j, pt: (b, 0, 0)
    pg5 = lambda b, j, pt: (0, pt[b, n_pages - 1 - j], 0, 0, 0)
    pg4 = lambda b, j, pt: (0, pt[b, n_pages - 1 - j], 0, 0)
    in_specs = [pl.BlockSpec((None, N_HEADS, HEAD_DIM), tok), pl.BlockSpec((None, N_HEADS, HEAD_DIM), tok),
                pl.BlockSpec((None, N_HEADS, HEAD_DIM), tok), pl.BlockSpec((None, N_HEADS, 1), tok),
                pl.BlockSpec((None, None, page, N_HEADS, HEAD_DIM), pg5),
                pl.BlockSpec((None, None, page, N_HEADS, HEAD_DIM), pg5),
                pl.BlockSpec((None, None, page, N_HEADS), pg4)]
    grid_spec = pltpu.PrefetchScalarGridSpec(
        num_scalar_prefetch=1, grid=(nb, n_pages), in_specs=in_specs,
        out_specs=pl.BlockSpec((None, N_HEADS, HEAD_DIM), tok),
        scratch_shapes=[pltpu.VMEM((N_HEADS, 1), F32), pltpu.VMEM((N_HEADS, 1), F32),
                        pltpu.VMEM((N_HEADS, HEAD_DIM), F32), pltpu.VMEM((N_HEADS, 1), F32)])
    return pl.pallas_call(functools.partial(_fox_decode_kernel, page=page), grid_spec=grid_spec,
                          out_shape=jax.ShapeDtypeStruct((nb, N_HEADS, HEAD_DIM), F32),
                          compiler_params=_params("arbitrary", "arbitrary"), name="fox_decode")(
        page_table, q, kn, vn, lfn, cache_k, cache_v, cache_lf)


def _moba_decode_kernel(pt_ref, q_ref, kn_ref, vn_ref, tab_ref, k_ref, v_ref, o_ref,
                        ksum_ref, sel_ref, bias_ref, m_ref, l_ref, acc_ref, *, page, n_pages):
    phase = pl.program_id(1)
    j = pl.program_id(2)
    pages_per_block = MOBA_BLOCK // page
    nblk = n_pages // pages_per_block
    p_len = n_pages * page
    q = q_ref[...]
    n = j // pages_per_block

    @pl.when(phase == 0)
    def _():
        @pl.when(j == 0)
        def _():
            ksum_ref[...] = jnp.zeros_like(ksum_ref)
        ksum_ref[pl.ds(n, 1)] += jnp.sum(k_ref[...], axis=0)[None]

    @pl.when(phase == 1)
    def _():
        qs = q * SCALE

        @pl.when(j == 0)
        def _():
            kmean = ksum_ref[...] * (1.0 / MOBA_BLOCK)
            g = jnp.sum(kmean * q[None], axis=-1, keepdims=True)
            idx = lax.broadcasted_iota(jnp.int32, g.shape, 0)
            sel = jnp.zeros(g.shape, F32)
            for _ in range(min(MOBA_TOPK, nblk)):
                mx = jnp.max(g, axis=0, keepdims=True)
                first = jnp.min(jnp.where(g == mx, idx, nblk), axis=0, keepdims=True)
                pick = idx == first
                sel = jnp.where(pick, 1.0, sel)
                g = jnp.where(pick, -jnp.inf, g)
            sel_ref[...] = sel
            m_ref[...] = jnp.sum(qs * kn_ref[...], axis=-1, keepdims=True) + tab_ref[0]
            l_ref[...] = jnp.ones_like(l_ref)
            acc_ref[...] = vn_ref[...]

        first_dist = p_len - (j * page + page - 1)

        @pl.when(first_dist >= T5_THRESHOLDS[-1])
        def _():
            bias_ref[...] = jnp.broadcast_to(tab_ref[T5_BUCKETS - 1][None], bias_ref.shape)

        @pl.when(first_dist < T5_THRESHOLDS[-1])
        def _():
            dist = p_len - j * page - lax.broadcasted_iota(jnp.int32, (page, N_HEADS, 1), 0)
            bias_ref[...] = _t5_select(dist, lambda bkt: tab_ref[bkt][None])

        s = jnp.sum(k_ref[...] * qs[None], axis=-1, keepdims=True) + bias_ref[...]
        s = jnp.where(sel_ref[pl.ds(n, 1)] > 0.5, s, NEG)
        _decode_update(m_ref, l_ref, acc_ref, s, v_ref[...])

        @pl.when(j == n_pages - 1)
        def _():
            o_ref[...] = acc_ref[...] / l_ref[...]


def _moba_decode(page_table, q, kn, vn, tab3, cache_k, cache_v):
    nb, n_pages = page_table.shape
    page = cache_k.shape[2]
    nblk = n_pages // (MOBA_BLOCK // page)
    tok = lambda b, ph, j, pt: (b, 0, 0)
    in_specs = [pl.BlockSpec((None, N_HEADS, HEAD_DIM), tok), pl.BlockSpec((None, N_HEADS, HEAD_DIM), tok),
                pl.BlockSpec((None, N_HEADS, HEAD_DIM), tok),
                pl.BlockSpec((T5_BUCKETS, N_HEADS, 1), lambda b, ph, j, pt: (0, 0, 0)),
                pl.BlockSpec((None, None, page, N_HEADS, HEAD_DIM), lambda b, ph, j, pt: (0, pt[b, j], 0, 0, 0)),
                pl.BlockSpec((None, None, page, N_HEADS, HEAD_DIM),
                             lambda b, ph, j, pt: (0, pt[b, j * ph], 0, 0, 0))]
    grid_spec = pltpu.PrefetchScalarGridSpec(
        num_scalar_prefetch=1, grid=(nb, 2, n_pages), in_specs=in_specs,
        out_specs=pl.BlockSpec((None, N_HEADS, HEAD_DIM), tok),
        scratch_shapes=[pltpu.VMEM((nblk, N_HEADS, HEAD_DIM), F32), pltpu.VMEM((nblk, N_HEADS, 1), F32),
                        pltpu.VMEM((page, N_HEADS, 1), F32),
                        pltpu.VMEM((N_HEADS, 1), F32), pltpu.VMEM((N_HEADS, 1), F32),
                        pltpu.VMEM((N_HEADS, HEAD_DIM), F32)])
    return pl.pallas_call(functools.partial(_moba_decode_kernel, page=page, n_pages=n_pages), grid_spec=grid_spec,
                          out_shape=jax.ShapeDtypeStruct((nb, N_HEADS, HEAD_DIM), F32),
                          compiler_params=_params("arbitrary", "arbitrary", "arbitrary"), name="moba_decode")(
        page_table, q, kn, vn, tab3, cache_k, cache_v)


def _heads(a, batch, seq):
    return a.reshape(batch, seq, N_HEADS, HEAD_DIM).transpose(0, 2, 1, 3)


def kernel(x_prompt, x_sample, cache_fox_k, cache_fox_v, cache_fox_logf, cache_moba_k, cache_moba_v, page_table,
           g_attn, w_in, b_forget, g_q_fox, g_k_fox, g_q_moba, g_k_moba, t5_table, w_branch_fox, w_branch_moba,
           w_out, g_ffn, w_router_group, w_router_expert, w_e_gate, w_e_up, w_e_down):
    depth = w_in.shape[0]
    assert depth == 1, "single-layer trunk"
    batch, seq, _ = x_prompt.shape
    nb, nt, _ = x_sample.shape
    assert nt == 1 and seq % (2 * MOBA_BLOCK) == 0
    n_p, n_s = batch * seq, nb * nt

    w = w_in[0]
    f0 = 3 * W_HEADS
    w_main = jnp.concatenate([w[:, :f0], w[:, f0 + N_HEADS:]], axis=1).astype(BF16)
    w_f = jnp.pad(w[:, f0:f0 + N_HEADS], ((0, 0), (0, LANES - N_HEADS))).astype(BF16)
    b_f = jnp.pad(b_forget[0], (0, LANES - N_HEADS))[None]
    gains = jnp.stack([jnp.tile(g[0], N_HEADS) for g in (g_q_fox, g_k_fox, g_q_moba, g_k_moba)])
    gains = jnp.pad(gains, ((0, 4), (0, 0)))
    seg = (jnp.arange(W_HEADS)[:, None] // HEAD_DIM == jnp.arange(W_HEADS)[None] // HEAD_DIM)
    seg = (seg.astype(F32) / HEAD_DIM).astype(BF16)
    wbf, wbm, wout = w_branch_fox[0].astype(BF16), w_branch_moba[0].astype(BF16), w_out[0].astype(BF16)
    w_router = jnp.concatenate([w_router_group[0], w_router_expert[0].reshape(D_MODEL, N_EXPERTS)], axis=1)
    w_router = jnp.pad(w_router, ((0, 0), (0, LANES - N_GROUPS - N_EXPERTS)))
    wg, wu, wd = w_e_gate[0].astype(BF16), w_e_up[0].astype(BF16), w_e_down[0].astype(BF16)
    proj_w = (g_attn, w_main, w_f, b_f, gains, seg)

    xp = x_prompt.reshape(n_p, D_MODEL)
    qa, ka, va, lf, qb, kb, vb, sga, sgb, kmean = _project(xp, MOBA_BLOCK, *proj_w)
    cum = _cumsum(lf, batch, seq)[:, :N_HEADS].reshape(batch, seq, N_HEADS)
    cq = cum.transpose(0, 2, 1)[..., None]
    ck = cum.transpose(0, 2, 1)[:, :, None, :]
    q_f = (_heads(qa, batch, seq) * SCALE).astype(BF16)
    kt_f = _heads(ka, batch, seq).transpose(0, 1, 3, 2).astype(BF16)
    v_f = _heads(va, batch, seq).astype(BF16)
    oa = _fox_prompt(q_f, kt_f, v_f, cq, ck)
    q32_m = _heads(qb, batch, seq)
    q_m = (q32_m * SCALE).astype(BF16)
    kt_m = _heads(kb, batch, seq).transpose(0, 1, 3, 2).astype(BF16)
    v_m = _heads(vb, batch, seq).astype(BF16)
    kmt = kmean.reshape(batch, seq // MOBA_BLOCK, N_HEADS, HEAD_DIM).transpose(0, 2, 3, 1)
    ob = _moba_prompt(q_m, q32_m, kt_m, v_m, kmt, _moba_bias(t5_table))
    unheads = lambda o: o.transpose(0, 2, 1, 3).reshape(n_p, W_HEADS)
    merge_w = (wbf, wbm, wout, g_ffn, w_router)
    x1, h2, route = _merge(xp, unheads(oa), unheads(ob), sga, sgb, *merge_w, tm=MOBA_BLOCK)
    y_prompt = _moe(x1, h2, route, wg, wu, wd, tm=min(1024, n_p)).reshape(batch, seq, D_MODEL)

    xs = x_sample.reshape(n_s, D_MODEL)
    qa_s, ka_s, va_s, lf_s, qb_s, kb_s, vb_s, sga_s, sgb_s, _ = _project(xs, n_s, *proj_w)
    tokh = lambda a: a.reshape(n_s, N_HEADS, HEAD_DIM)
    oa_s = _fox_decode(page_table, tokh(qa_s), tokh(ka_s), tokh(va_s), lf_s[:, :N_HEADS, None],
                       cache_fox_k, cache_fox_v, cache_fox_logf)
    ob_s = _moba_decode(page_table, tokh(qb_s), tokh(kb_s), tokh(vb_s), t5_table[:, :, None],
                        cache_moba_k, cache_moba_v)
    flat = lambda o: o.reshape(n_s, W_HEADS).astype(BF16)
    x1_s, h2_s, route_s = _merge(xs, flat(oa_s), flat(ob_s), sga_s, sgb_s, *merge_w, tm=n_s)
    y_sample = _moe(x1_s, h2_s, route_s, wg, wu, wd, tm=n_s).reshape(nb, nt, D_MODEL)

    kv_p = lambda a: a.reshape(depth, batch, seq, N_HEADS, HEAD_DIM)
    kv_s = lambda a: a.reshape(depth, nb, nt, N_HEADS, HEAD_DIM)
    return (y_prompt, y_sample,
            kv_p(ka), kv_p(va), lf[:, :N_HEADS].reshape(depth, batch, seq, N_HEADS), kv_p(kb), kv_p(vb),
            kv_s(ka_s), kv_s(va_s), lf_s[:, :N_HEADS].reshape(depth, nb, nt, N_HEADS), kv_s(kb_s), kv_s(vb_s))
```

```python
import functools
import math

import numpy as np
import jax
import jax.numpy as jnp
from jax import lax
from jax.experimental import pallas as pl
from jax.experimental.pallas import tpu as pltpu

F32 = jnp.float32
BF16 = jnp.bfloat16
HIGHEST = lax.Precision.HIGHEST

D_MODEL = 1024
N_HEADS = 8
HEAD_DIM = 64
W_HEADS = N_HEADS * HEAD_DIM
MOBA_BLOCK = 256
MOBA_TOPK = 3
T5_BUCKETS = 32
T5_MAX_DIST = 128
N_GROUPS = 4
EXPERTS_PER_GROUP = 8
N_EXPERTS = N_GROUPS * EXPERTS_PER_GROUP
D_EXPERT = 256
EPS = 1e-6
NEG = -1e30
SCALE = HEAD_DIM ** -0.5
LANES = 128
VMEM_LIMIT = 56 * 1024 * 1024


def _t5_thresholds():
    max_exact = T5_BUCKETS // 2
    n = np.arange(0, 4 * T5_MAX_DIST)
    nf = np.maximum(n, max_exact).astype(np.float32)
    large = max_exact + (np.log(nf / np.float32(max_exact)) / np.float32(math.log(T5_MAX_DIST / max_exact))
                         * np.float32(T5_BUCKETS - max_exact)).astype(np.int32)
    bucket = np.where(n < max_exact, n, np.minimum(large, T5_BUCKETS - 1))
    return [int(np.argmax(bucket >= j)) for j in range(1, T5_BUCKETS)]


T5_THRESHOLDS = _t5_thresholds()


def _params(*sem):
    return pltpu.CompilerParams(dimension_semantics=sem, vmem_limit_bytes=VMEM_LIMIT)


def _proj_kernel(x_ref, g_ref, w_ref, wf_ref, bf_ref, gains_ref, seg_ref,
                 qa_ref, ka_ref, va_ref, lf_ref, qb_ref, kb_ref, vb_ref, sga_ref, sgb_ref, km_ref):
    x = x_ref[...]
    h = x * lax.rsqrt(jnp.mean(x * x, axis=-1, keepdims=True) + EPS) * g_ref[...]
    hb = h.astype(BF16)
    seg = seg_ref[...]

    def proj(i, width=W_HEADS):
        return jnp.dot(hb, w_ref[:, i:i + width], preferred_element_type=F32)

    def headnorm(z, row):
        ms = jnp.dot((z * z).astype(BF16), seg, preferred_element_type=F32)
        return z * lax.rsqrt(ms + EPS) * gains_ref[row:row + 1, :]

    qa_ref[...] = headnorm(proj(0), 0)
    ka_ref[...] = headnorm(proj(W_HEADS), 1)
    va_ref[...] = proj(2 * W_HEADS)
    qb_ref[...] = headnorm(proj(3 * W_HEADS), 2)
    kb = headnorm(proj(4 * W_HEADS), 3)
    kb_ref[...] = kb
    km_ref[...] = jnp.mean(kb, axis=0, keepdims=True)
    vb_ref[...] = proj(5 * W_HEADS)
    sga_ref[...] = jax.nn.sigmoid(proj(6 * W_HEADS, D_MODEL)).astype(BF16)
    sgb_ref[...] = jax.nn.sigmoid(proj(6 * W_HEADS + D_MODEL, D_MODEL)).astype(BF16)
    f = jnp.dot(hb, wf_ref[...], preferred_element_type=F32) + bf_ref[...]
    lf_ref[...] = jnp.minimum(f, 0.0) - jnp.log1p(jnp.exp(-jnp.abs(f)))


def _project(x2d, tm, g_attn, w_main, w_f, b_f, gains, seg):
    n = x2d.shape[0]
    row = lambda i: (i, 0)
    const = lambda i: (0, 0)
    wide = lambda w, dt: jax.ShapeDtypeStruct((n, w), dt)
    out_shape = (wide(W_HEADS, F32), wide(W_HEADS, F32), wide(W_HEADS, F32), wide(LANES, F32),
                 wide(W_HEADS, F32), wide(W_HEADS, F32), wide(W_HEADS, F32),
                 wide(D_MODEL, BF16), wide(D_MODEL, BF16),
                 jax.ShapeDtypeStruct((n // tm, 1, W_HEADS), F32))
    blk = lambda w: pl.BlockSpec((tm, w), row)
    out_specs = (blk(W_HEADS), blk(W_HEADS), blk(W_HEADS), blk(LANES), blk(W_HEADS), blk(W_HEADS), blk(W_HEADS),
                 blk(D_MODEL), blk(D_MODEL), pl.BlockSpec((None, 1, W_HEADS), lambda i: (i, 0, 0)))
    in_specs = [blk(D_MODEL), pl.BlockSpec((1, D_MODEL), const), pl.BlockSpec(w_main.shape, const),
                pl.BlockSpec(w_f.shape, const), pl.BlockSpec((1, LANES), const),
                pl.BlockSpec(gains.shape, const), pl.BlockSpec(seg.shape, const)]
    return pl.pallas_call(_proj_kernel, grid=(n // tm,), in_specs=in_specs, out_specs=out_specs,
                          out_shape=out_shape, compiler_params=_params("arbitrary"), name="proj")(
        x2d, g_attn, w_main, w_f, b_f, gains, seg)


def _cumsum_kernel(lf_ref, o_ref, carry_ref, *, tc):
    @pl.when(pl.program_id(1) == 0)
    def _():
        carry_ref[...] = jnp.zeros_like(carry_ref)

    r = lax.broadcasted_iota(jnp.int32, (tc, tc), 0)
    c = lax.broadcasted_iota(jnp.int32, (tc, tc), 1)
    tri = jnp.where(c <= r, 1.0, 0.0).astype(F32)
    out = jnp.dot(tri, lf_ref[...], preferred_element_type=F32, precision=HIGHEST) + carry_ref[...]
    o_ref[...] = out
    carry_ref[...] = out[tc - 1:tc, :]


def _cumsum(lf, batch, seq, tc=256):
    nchunk = seq // tc
    spec = pl.BlockSpec((tc, LANES), lambda b, i: (b * nchunk + i, 0))
    return pl.pallas_call(functools.partial(_cumsum_kernel, tc=tc), grid=(batch, nchunk),
                          in_specs=[spec], out_specs=spec,
                          out_shape=jax.ShapeDtypeStruct(lf.shape, F32),
                          scratch_shapes=[pltpu.VMEM((1, LANES), F32)],
                          compiler_params=_params("arbitrary", "arbitrary"), name="logf_cumsum")(lf)


def _online_update(carry, s, v):
    m, l, acc = carry
    m_new = jnp.maximum(m, jnp.max(s, axis=-1, keepdims=True))
    alpha = jnp.exp(m - m_new)
    p = jnp.exp(s - m_new)
    l = alpha * l + jnp.sum(p, axis=-1, keepdims=True)
    acc = alpha * acc + jnp.dot(p.astype(BF16), v, preferred_element_type=F32)
    return m_new, l, acc


def _fox_kernel(q_ref, kt_ref, v_ref, cq_ref, ck_ref, o_ref, *, tq):
    qi = pl.program_id(2)
    q = q_ref[...]
    cq = cq_ref[...]

    def logits(c):
        k0 = pl.multiple_of(c * tq, tq)
        s = jnp.dot(q, kt_ref[:, pl.ds(k0, tq)], preferred_element_type=F32)
        return s + cq - ck_ref[:, pl.ds(k0, tq)], v_ref[pl.ds(k0, tq), :]

    def body(c, carry):
        s, v = logits(c)
        return _online_update(carry, s, v)

    init = (jnp.full((tq, 1), NEG, F32), jnp.zeros((tq, 1), F32), jnp.zeros((tq, HEAD_DIM), F32))
    carry = lax.fori_loop(0, qi, body, init)
    s, v = logits(qi)
    row = lax.broadcasted_iota(jnp.int32, (tq, tq), 0)
    col = lax.broadcasted_iota(jnp.int32, (tq, tq), 1)
    m, l, acc = _online_update(carry, jnp.where(col <= row, s, NEG), v)
    o_ref[...] = (acc / l).astype(o_ref.dtype)


def _fox_prompt(q, kt, v, cq, ck, tq=512):
    b, h, s, _ = q.shape
    grid = (b, h, s // tq)
    in_specs = [pl.BlockSpec((None, None, tq, HEAD_DIM), lambda b, h, i: (b, h, i, 0)),
                pl.BlockSpec((None, None, HEAD_DIM, s), lambda b, h, i: (b, h, 0, 0)),
                pl.BlockSpec((None, None, s, HEAD_DIM), lambda b, h, i: (b, h, 0, 0)),
                pl.BlockSpec((None, None, tq, 1), lambda b, h, i: (b, h, i, 0)),
                pl.BlockSpec((None, None, 1, s), lambda b, h, i: (b, h, 0, 0))]
    out_specs = pl.BlockSpec((None, None, tq, HEAD_DIM), lambda b, h, i: (b, h, i, 0))
    return pl.pallas_call(functools.partial(_fox_kernel, tq=tq), grid=grid, in_specs=in_specs, out_specs=out_specs,
                          out_shape=jax.ShapeDtypeStruct((b, h, s, HEAD_DIM), BF16),
                          compiler_params=_params("arbitrary", "arbitrary", "arbitrary"), name="fox_prompt")(
        q, kt, v, cq, ck)


def _t5_select(dist, table_at):
    bias = table_at(0)
    for j, thr in enumerate(T5_THRESHOLDS, start=1):
        bias = jnp.where(dist >= thr, table_at(j), bias)
    return bias


def _bias_kernel(tab_ref, o_ref):
    hh = pl.program_id(0)
    i = lax.broadcasted_iota(jnp.int32, (MOBA_BLOCK, 2 * MOBA_BLOCK), 0)
    j = lax.broadcasted_iota(jnp.int32, (MOBA_BLOCK, 2 * MOBA_BLOCK), 1)
    dist = MOBA_BLOCK + i - j
    o_ref[...] = _t5_select(dist, lambda bkt: tab_ref[bkt, hh])


def _moba_bias(t5_table):
    return pl.pallas_call(_bias_kernel, grid=(N_HEADS,),
                          in_specs=[pl.BlockSpec(memory_space=pltpu.SMEM)],
                          out_specs=pl.BlockSpec((None, MOBA_BLOCK, 2 * MOBA_BLOCK), lambda h: (h, 0, 0)),
                          out_shape=jax.ShapeDtypeStruct((N_HEADS, MOBA_BLOCK, 2 * MOBA_BLOCK), F32),
                          compiler_params=_params("arbitrary"), name="moba_bias")(t5_table)


def _top_blocks(gate, n_eligible):
    lane = lax.broadcasted_iota(jnp.int32, gate.shape, 1)
    g = jnp.where(lane < n_eligible, gate, -jnp.inf)
    sel = jnp.zeros(gate.shape, F32)
    for _ in range(MOBA_TOPK):
        mx = jnp.max(g, axis=-1, keepdims=True)
        idx = jnp.min(jnp.where(g == mx, lane, gate.shape[1]), axis=-1, keepdims=True)
        pick = jnp.logical_and(lane == idx, mx > -jnp.inf)
        sel = jnp.where(pick, 1.0, sel)
        g = jnp.where(pick, -jnp.inf, g)
    return sel


def _moba_kernel(q_ref, q32_ref, kt_ref, v_ref, kmt_ref, bias_ref, o_ref, *, nb):
    blk = MOBA_BLOCK
    qi = pl.program_id(2)
    q = q_ref[...]
    gate = jnp.dot(q32_ref[...], kmt_ref[...], preferred_element_type=F32, precision=HIGHEST)
    sel = _top_blocks(gate, qi).astype(BF16)
    far_bias = bias_ref[blk - 1:blk, 0:1]

    k0 = pl.multiple_of(qi * blk, blk)
    s = jnp.dot(q, kt_ref[:, pl.ds(k0, blk)], preferred_element_type=F32) + bias_ref[:, blk:]
    row = lax.broadcasted_iota(jnp.int32, (blk, blk), 0)
    col = lax.broadcasted_iota(jnp.int32, (blk, blk), 1)
    init = (jnp.full((blk, 1), NEG, F32), jnp.zeros((blk, 1), F32), jnp.zeros((blk, HEAD_DIM), F32))
    carry = _online_update(init, jnp.where(col <= row, s, NEG), v_ref[pl.ds(k0, blk), :])

    def body(n, carry):
        k0 = pl.multiple_of(n * blk, blk)
        s = jnp.dot(q, kt_ref[:, pl.ds(k0, blk)], preferred_element_type=F32)
        s = s + jnp.where(n == qi - 1, bias_ref[:, :blk], far_bias)
        onehot = (lax.broadcasted_iota(jnp.int32, (nb, blk), 0) == n).astype(BF16)
        chosen = jnp.dot(sel, onehot, preferred_element_type=F32)
        return _online_update(carry, jnp.where(chosen > 0.5, s, NEG), v_ref[pl.ds(k0, blk), :])

    m, l, acc = lax.fori_loop(0, qi, body, carry)
    o_ref[...] = (acc / l).astype(o_ref.dtype)


def _moba_prompt(q, q32, kt, v, kmt, bias):
    b, h, s, _ = q.shape
    blk = MOBA_BLOCK
    nb = s // blk
    qspec = pl.BlockSpec((None, None, blk, HEAD_DIM), lambda b, h, i: (b, h, i, 0))
    in_specs = [qspec, qspec,
                pl.BlockSpec((None, None, HEAD_DIM, s), lambda b, h, i: (b, h, 0, 0)),
                pl.BlockSpec((None, None, s, HEAD_DIM), lambda b, h, i: (b, h, 0, 0)),
                pl.BlockSpec((None, None, HEAD_DIM, nb), lambda b, h, i: (b, h, 0, 0)),
                pl.BlockSpec((None, blk, 2 * blk), lambda b, h, i: (h, 0, 0))]
    return pl.pallas_call(functools.partial(_moba_kernel, nb=nb), grid=(b, h, nb), in_specs=in_specs,
                          out_specs=qspec, out_shape=jax.ShapeDtypeStruct((b, h, s, HEAD_DIM), BF16),
                          compiler_params=_params("arbitrary", "arbitrary", "arbitrary"), name="moba_prompt")(
        q, q32, kt, v, kmt, bias)


def _merge_kernel(x_ref, oa_ref, ob_ref, sga_ref, sgb_ref, wbf_ref, wbm_ref, wout_ref, g_ref, wr_ref,
                  x1_ref, h2_ref, route_ref):
    ma = jnp.dot(oa_ref[...], wbf_ref[...], preferred_element_type=F32)
    mb = jnp.dot(ob_ref[...], wbm_ref[...], preferred_element_type=F32)
    mix = sga_ref[...].astype(F32) * ma + sgb_ref[...].astype(F32) * mb
    x1 = x_ref[...] + jnp.dot(mix.astype(BF16), wout_ref[...], preferred_element_type=F32)
    x1_ref[...] = x1
    h2 = x1 * lax.rsqrt(jnp.mean(x1 * x1, axis=-1, keepdims=True) + EPS) * g_ref[...]
    h2_ref[...] = h2.astype(BF16)

    logits = jnp.dot(h2, wr_ref[...], preferred_element_type=F32, precision=HIGHEST)
    lane = lax.broadcasted_iota(jnp.int32, logits.shape, 1)
    big = LANES

    def argmax_low(vals):
        mx = jnp.max(vals, axis=-1, keepdims=True)
        return mx, jnp.min(jnp.where(vals == mx, lane, big), axis=-1, keepdims=True)

    glog = jnp.where(lane < N_GROUPS, logits, -jnp.inf)
    gmax, gsel = argmax_low(glog)
    gprob = 1.0 / jnp.sum(jnp.exp(glog - gmax), axis=-1, keepdims=True)
    first = N_GROUPS + gsel * EXPERTS_PER_GROUP
    in_group = jnp.logical_and(lane >= first, lane < first + EXPERTS_PER_GROUP)
    elog = jnp.where(in_group, logits, -jnp.inf)
    v0, i0 = argmax_low(elog)
    v1, i1 = argmax_low(jnp.where(lane == i0, -jnp.inf, elog))
    e1 = jnp.exp(v1 - v0)
    w0 = gprob / (1.0 + e1)
    w1 = gprob * e1 / (1.0 + e1)
    route = jnp.where(lane == 0, (i0 - N_GROUPS).astype(F32),
                      jnp.where(lane == 1, (i1 - N_GROUPS).astype(F32),
                                jnp.where(lane == 2, w0, jnp.where(lane == 3, w1, 0.0))))
    route_ref[...] = route


def _merge(x2d, oa, ob, sga, sgb, wbf, wbm, wout, g_ffn, w_router, tm):
    n = x2d.shape[0]
    row = lambda i: (i, 0)
    const = lambda i: (0, 0)
    blk = lambda w: pl.BlockSpec((tm, w), row)
    full = lambda a: pl.BlockSpec(a.shape, const)
    in_specs = [blk(D_MODEL), blk(W_HEADS), blk(W_HEADS), blk(D_MODEL), blk(D_MODEL),
                full(wbf), full(wbm), full(wout), full(g_ffn), full(w_router)]
    out_shape = (jax.ShapeDtypeStruct((n, D_MODEL), F32), jax.ShapeDtypeStruct((n, D_MODEL), BF16),
                 jax.ShapeDtypeStruct((n, LANES), F32))
    return pl.pallas_call(_merge_kernel, grid=(n // tm,), in_specs=in_specs,
                          out_specs=(blk(D_MODEL), blk(D_MODEL), blk(LANES)), out_shape=out_shape,
                          compiler_params=_params("arbitrary"), name="merge_router")(
        x2d, oa, ob, sga, sgb, wbf, wbm, wout, g_ffn, w_router)


def _moe_kernel(x1_ref, h2_ref, route_ref, wg_ref, wu_ref, wd_ref, o_ref):
    e = pl.program_id(1)

    @pl.when(e == 0)
    def _():
        o_ref[...] = x1_ref[...]

    h = h2_ref[...]
    a = jax.nn.silu(jnp.dot(h, wg_ref[...], preferred_element_type=F32)) * \
        jnp.dot(h, wu_ref[...], preferred_element_type=F32)
    y = jnp.dot(a.astype(BF16), wd_ref[...], preferred_element_type=F32)
    r = route_ref[...]
    ef = e.astype(F32)
    gate = jnp.where(r[:, 0:1] == ef, r[:, 2:3], 0.0) + jnp.where(r[:, 1:2] == ef, r[:, 3:4], 0.0)
    o_ref[...] += y * gate


def _moe(x1, h2, route, wg, wu, wd, tm):
    n = x1.shape[0]
    row = lambda i, e: (i, 0)
    in_specs = [pl.BlockSpec((tm, D_MODEL), row), pl.BlockSpec((tm, D_MODEL), row), pl.BlockSpec((tm, LANES), row),
                pl.BlockSpec((None, D_MODEL, D_EXPERT), lambda i, e: (e, 0, 0)),
                pl.BlockSpec((None, D_MODEL, D_EXPERT), lambda i, e: (e, 0, 0)),
                pl.BlockSpec((None, D_EXPERT, D_MODEL), lambda i, e: (e, 0, 0))]
    return pl.pallas_call(_moe_kernel, grid=(n // tm, N_EXPERTS), in_specs=in_specs,
                          out_specs=pl.BlockSpec((tm, D_MODEL), row),
                          out_shape=jax.ShapeDtypeStruct((n, D_MODEL), F32),
                          compiler_params=_params("arbitrary", "arbitrary"), name="experts")(
        x1, h2, route, wg, wu, wd)


DECODE_PAGES_PER_STEP = 4


def _page_logits(kt, q3):
    return jnp.sum(kt * q3, axis=1)


def _fox_decode_kernel(pt_ref, q_ref, kn_ref, vn_ref, lfn_ref, *rest, page, group):
    k_refs, v_refs, lf_refs = rest[:group], rest[group:2 * group], rest[2 * group:3 * group]
    o_ref, m_ref, l_ref, acc_ref, carry_ref = rest[3 * group:]
    j = pl.program_id(1)
    q3 = q_ref[...] * SCALE

    @pl.when(j == 0)
    def _():
        m_ref[...] = jnp.sum(q3 * kn_ref[...], axis=1)
        l_ref[...] = jnp.ones_like(l_ref)
        lane = lax.broadcasted_iota(jnp.int32, acc_ref.shape, 2)
        acc_ref[...] = jnp.where(lane == 0, vn_ref[...], 0.0)
        carry_ref[...] = lfn_ref[...]

    r0 = lax.broadcasted_iota(jnp.int32, (page, page), 0)
    r1 = lax.broadcasted_iota(jnp.int32, (page, page), 1)
    later = jnp.where(r0 > r1, 1.0, 0.0).astype(F32)
    for g in range(group):
        lf = lf_refs[g][...]
        decay = jnp.dot(lf, later, preferred_element_type=F32, precision=HIGHEST) + carry_ref[...]
        s = _page_logits(k_refs[g][...], q3) + decay
        m = m_ref[...]
        m_new = jnp.maximum(m, jnp.max(s, axis=-1, keepdims=True))
        alpha = jnp.exp(m - m_new)
        p = jnp.exp(s - m_new)
        l_ref[...] = alpha * l_ref[...] + jnp.sum(p, axis=-1, keepdims=True)
        acc_ref[...] = alpha[:, :, None] * acc_ref[...] + v_refs[g][...] * p[:, None, :]
        m_ref[...] = m_new
        carry_ref[...] = carry_ref[...] + jnp.sum(lf, axis=-1, keepdims=True)

    @pl.when(j == pl.num_programs(1) - 1)
    def _():
        o_ref[...] = jnp.sum(acc_ref[...], axis=-1, keepdims=True) / l_ref[...][:, :, None]


def _fox_decode(page_table, q, kn, vn, lfn, kt_cache, vt_cache, lft_cache):
    nb, n_pages = page_table.shape
    page = kt_cache.shape[-1]
    group = math.gcd(DECODE_PAGES_PER_STEP, n_pages)
    tok4 = lambda b, j, pt: (b, 0, 0, 0)
    tok3 = lambda b, j, pt: (b, 0, 0)

    def paged(g, ndim):
        return lambda b, j, pt: (pt[b, n_pages - 1 - (j * group + g)],) + (0,) * ndim

    col = pl.BlockSpec((None, N_HEADS, HEAD_DIM, 1), tok4)
    kv_specs = [pl.BlockSpec((None, N_HEADS, HEAD_DIM, page), paged(g, 3)) for g in range(group)]
    lf_specs = [pl.BlockSpec((None, N_HEADS, page), paged(g, 2)) for g in range(group)]
    in_specs = [col, col, col, pl.BlockSpec((None, N_HEADS, 1), tok3)] + kv_specs + kv_specs + lf_specs
    grid_spec = pltpu.PrefetchScalarGridSpec(
        num_scalar_prefetch=1, grid=(nb, n_pages // group), in_specs=in_specs, out_specs=col,
        scratch_shapes=[pltpu.VMEM((N_HEADS, 1), F32), pltpu.VMEM((N_HEADS, 1), F32),
                        pltpu.VMEM((N_HEADS, HEAD_DIM, page), F32), pltpu.VMEM((N_HEADS, 1), F32)])
    return pl.pallas_call(functools.partial(_fox_decode_kernel, page=page, group=group), grid_spec=grid_spec,
                          out_shape=jax.ShapeDtypeStruct((nb, N_HEADS, HEAD_DIM, 1), F32),
                          compiler_params=_params("arbitrary", "arbitrary"), name="fox_decode")(
        page_table, q, kn, vn, lfn, *([kt_cache] * group), *([vt_cache] * group), *([lft_cache] * group))


def _moba_score_kernel(pt_ref, q_ref, kn_ref, vn_ref, tab_ref, *rest, page, n_pages, group):
    k_refs = rest[:group]
    p_ref, sel_ref, own_ref, s_ref = rest[group:]
    j = pl.program_id(1)
    ppb = MOBA_BLOCK // page
    nblk = n_pages // ppb
    p_len = n_pages * page
    q3 = q_ref[...]
    for g in range(group):
        s_ref[pl.ds(j * group + g, 1)] = _page_logits(k_refs[g][...], q3)[None]

    @pl.when(j == pl.num_programs(1) - 1)
    def _():
        s = s_ref[...].reshape(nblk, ppb, N_HEADS, page)
        gate = jnp.sum(jnp.sum(s, axis=1), axis=-1, keepdims=True) * (1.0 / MOBA_BLOCK)
        idx = lax.broadcasted_iota(jnp.int32, gate.shape, 0)
        lane = lax.broadcasted_iota(jnp.int32, (N_HEADS, LANES), 1)
        chosen = jnp.zeros(gate.shape, F32)
        picks = jnp.zeros((N_HEADS, LANES), jnp.int32)
        for r in range(min(MOBA_TOPK, nblk)):
            mx = jnp.max(gate, axis=0, keepdims=True)
            first = jnp.min(jnp.where(gate == mx, idx, nblk), axis=0, keepdims=True)
            pick = idx == first
            chosen = jnp.where(pick, 1.0, chosen)
            gate = jnp.where(pick, -jnp.inf, gate)
            picks = jnp.where(lane == r, first[0], picks)
        sel_ref[...] = picks

        logits = s * SCALE + tab_ref[T5_BUCKETS - 1][None, None]
        masked = jnp.where(chosen[:, None] > 0.5, logits, NEG).reshape(n_pages, N_HEADS, page)
        s_ref[...] = masked
        for pg in range(n_pages):
            if p_len - (pg * page + page - 1) < T5_THRESHOLDS[-1]:
                dist = p_len - pg * page - lax.broadcasted_iota(jnp.int32, (1, page), 1)
                bias = _t5_select(dist, lambda bkt: tab_ref[bkt])
                near = s[pg // ppb, pg % ppb] * SCALE + bias
                s_ref[pg] = jnp.where(chosen[pg // ppb] > 0.5, near, NEG)
        masked = s_ref[...]
        s_own = jnp.sum(q3 * kn_ref[...], axis=1) * SCALE + tab_ref[0]
        m = jnp.maximum(jnp.max(jnp.max(masked, axis=0), axis=-1, keepdims=True), s_own)
        p = jnp.exp(masked - m[None])
        p_own = jnp.exp(s_own - m)
        inv = 1.0 / (jnp.sum(jnp.sum(p, axis=0), axis=-1, keepdims=True) + p_own)
        p_ref[...] = p * inv[None]
        own_ref[...] = (p_own * inv)[:, :, None] * vn_ref[...]


def _moba_gather_kernel(vpage_ref, lpage_ref, *rest):
    v_refs, p_refs = rest[:N_HEADS], rest[N_HEADS:2 * N_HEADS]
    own_ref, o_ref, acc_ref = rest[2 * N_HEADS:]
    t = pl.program_id(1)

    @pl.when(t == 0)
    def _():
        acc_ref[...] = jnp.zeros_like(acc_ref)

    for h in range(N_HEADS):
        acc_ref[h] += v_refs[h][...] * p_refs[h][h:h + 1, :]

    @pl.when(t == pl.num_programs(1) - 1)
    def _():
        o_ref[...] = jnp.sum(acc_ref[...], axis=-1, keepdims=True) + own_ref[...]


def _moba_decode(page_table, q, kn, vn, tab3, kt_cache, vt_cache):
    nb, n_pages = page_table.shape
    page = kt_cache.shape[-1]
    ppb = MOBA_BLOCK // page
    nblk = n_pages // ppb
    ntop = min(MOBA_TOPK, nblk)
    group = math.gcd(DECODE_PAGES_PER_STEP, n_pages)
    tok4 = lambda b, j, pt: (b, 0, 0, 0)
    col = pl.BlockSpec((None, N_HEADS, HEAD_DIM, 1), tok4)
    k_specs = [pl.BlockSpec((None, N_HEADS, HEAD_DIM, page),
                            (lambda g: lambda b, j, pt: (pt[b, j * group + g], 0, 0, 0))(g)) for g in range(group)]
    in_specs = [col, col, col, pl.BlockSpec((T5_BUCKETS, N_HEADS, 1), lambda b, j, pt: (0, 0, 0))] + k_specs
    out_specs = (pl.BlockSpec((None, n_pages, N_HEADS, page), tok4),
                 pl.BlockSpec((None, N_HEADS, LANES), lambda b, j, pt: (b, 0, 0)), col)
    out_shape = (jax.ShapeDtypeStruct((nb, n_pages, N_HEADS, page), F32),
                 jax.ShapeDtypeStruct((nb, N_HEADS, LANES), jnp.int32),
                 jax.ShapeDtypeStruct((nb, N_HEADS, HEAD_DIM, 1), F32))
    grid_spec = pltpu.PrefetchScalarGridSpec(
        num_scalar_prefetch=1, grid=(nb, n_pages // group), in_specs=in_specs, out_specs=out_specs,
        scratch_shapes=[pltpu.VMEM((n_pages, N_HEADS, page), F32)])
    probs, picks, own = pl.pallas_call(
        functools.partial(_moba_score_kernel, page=page, n_pages=n_pages, group=group), grid_spec=grid_spec,
        out_shape=out_shape, compiler_params=_params("arbitrary", "arbitrary"), name="moba_score")(
        page_table, q, kn, vn, tab3, *([kt_cache] * group))

    lpage = (picks[:, :, :ntop, None] * ppb + jnp.arange(ppb)[None, None, None]).reshape(nb, N_HEADS, ntop * ppb)
    vpage = jnp.take_along_axis(page_table[:, None, :], lpage, axis=2)
    nstep = ntop * ppb
    lpage, vpage = lpage.reshape(nb, N_HEADS * nstep), vpage.reshape(nb, N_HEADS * nstep)
    v_specs = [pl.BlockSpec((None, None, HEAD_DIM, page),
                            (lambda h: lambda b, t, vp, lp: (vp[b, h * nstep + t], h, 0, 0))(h))
               for h in range(N_HEADS)]
    p_specs = [pl.BlockSpec((None, None, N_HEADS, page),
                            (lambda h: lambda b, t, vp, lp: (b, lp[b, h * nstep + t], 0, 0))(h))
               for h in range(N_HEADS)]
    col2 = pl.BlockSpec((None, N_HEADS, HEAD_DIM, 1), lambda b, t, vp, lp: (b, 0, 0, 0))
    grid_spec = pltpu.PrefetchScalarGridSpec(
        num_scalar_prefetch=2, grid=(nb, nstep), in_specs=v_specs + p_specs + [col2], out_specs=col2,
        scratch_shapes=[pltpu.VMEM((N_HEADS, HEAD_DIM, page), F32)])
    return pl.pallas_call(_moba_gather_kernel, grid_spec=grid_spec,
                          out_shape=jax.ShapeDtypeStruct((nb, N_HEADS, HEAD_DIM, 1), F32),
                          compiler_params=_params("arbitrary", "arbitrary"), name="moba_gather")(
        vpage, lpage, *([vt_cache] * N_HEADS), *([probs] * N_HEADS), own)


def _heads(a, batch, seq):
    return a.reshape(batch, seq, N_HEADS, HEAD_DIM).transpose(0, 2, 1, 3)


def kernel(x_prompt, x_sample, cache_fox_k, cache_fox_v, cache_fox_logf, cache_moba_k, cache_moba_v, page_table,
           g_attn, w_in, b_forget, g_q_fox, g_k_fox, g_q_moba, g_k_moba, t5_table, w_branch_fox, w_branch_moba,
           w_out, g_ffn, w_router_group, w_router_expert, w_e_gate, w_e_up, w_e_down):
    depth = w_in.shape[0]
    assert depth == 1, "single-layer trunk"
    batch, seq, _ = x_prompt.shape
    nb, nt, _ = x_sample.shape
    assert nt == 1 and seq % (2 * MOBA_BLOCK) == 0
    n_p, n_s = batch * seq, nb * nt

    w = w_in[0]
    f0 = 3 * W_HEADS
    w_main = jnp.concatenate([w[:, :f0], w[:, f0 + N_HEADS:]], axis=1).astype(BF16)
    w_f = jnp.pad(w[:, f0:f0 + N_HEADS], ((0, 0), (0, LANES - N_HEADS))).astype(BF16)
    b_f = jnp.pad(b_forget[0], (0, LANES - N_HEADS))[None]
    gains = jnp.stack([jnp.tile(g[0], N_HEADS) for g in (g_q_fox, g_k_fox, g_q_moba, g_k_moba)])
    gains = jnp.pad(gains, ((0, 4), (0, 0)))
    seg = (jnp.arange(W_HEADS)[:, None] // HEAD_DIM == jnp.arange(W_HEADS)[None] // HEAD_DIM)
    seg = (seg.astype(F32) / HEAD_DIM).astype(BF16)
    wbf, wbm, wout = w_branch_fox[0].astype(BF16), w_branch_moba[0].astype(BF16), w_out[0].astype(BF16)
    w_router = jnp.concatenate([w_router_group[0], w_router_expert[0].reshape(D_MODEL, N_EXPERTS)], axis=1)
    w_router = jnp.pad(w_router, ((0, 0), (0, LANES - N_GROUPS - N_EXPERTS)))
    wg, wu, wd = w_e_gate[0].astype(BF16), w_e_up[0].astype(BF16), w_e_down[0].astype(BF16)
    proj_w = (g_attn, w_main, w_f, b_f, gains, seg)

    xp = x_prompt.reshape(n_p, D_MODEL)
    qa, ka, va, lf, qb, kb, vb, sga, sgb, kmean = _project(xp, MOBA_BLOCK, *proj_w)
    cum = _cumsum(lf, batch, seq)[:, :N_HEADS].reshape(batch, seq, N_HEADS)
    cq = cum.transpose(0, 2, 1)[..., None]
    ck = cum.transpose(0, 2, 1)[:, :, None, :]
    q_f = (_heads(qa, batch, seq) * SCALE).astype(BF16)
    kt_f = _heads(ka, batch, seq).transpose(0, 1, 3, 2).astype(BF16)
    v_f = _heads(va, batch, seq).astype(BF16)
    oa = _fox_prompt(q_f, kt_f, v_f, cq, ck)
    q32_m = _heads(qb, batch, seq)
    q_m = (q32_m * SCALE).astype(BF16)
    kt_m = _heads(kb, batch, seq).transpose(0, 1, 3, 2).astype(BF16)
    v_m = _heads(vb, batch, seq).astype(BF16)
    kmt = kmean.reshape(batch, seq // MOBA_BLOCK, N_HEADS, HEAD_DIM).transpose(0, 2, 3, 1)
    ob = _moba_prompt(q_m, q32_m, kt_m, v_m, kmt, _moba_bias(t5_table))
    unheads = lambda o: o.transpose(0, 2, 1, 3).reshape(n_p, W_HEADS)
    merge_w = (wbf, wbm, wout, g_ffn, w_router)
    x1, h2, route = _merge(xp, unheads(oa), unheads(ob), sga, sgb, *merge_w, tm=MOBA_BLOCK)
    y_prompt = _moe(x1, h2, route, wg, wu, wd, tm=min(1024, n_p)).reshape(batch, seq, D_MODEL)

    xs = x_sample.reshape(n_s, D_MODEL)
    qa_s, ka_s, va_s, lf_s, qb_s, kb_s, vb_s, sga_s, sgb_s, _ = _project(xs, n_s, *proj_w)
    tokh = lambda a: a.reshape(n_s, N_HEADS, HEAD_DIM, 1)
    rows_last = lambda c: c[0].transpose(0, 2, 3, 1)
    oa_s = _fox_decode(page_table, tokh(qa_s), tokh(ka_s), tokh(va_s), lf_s[:, :N_HEADS, None],
                       rows_last(cache_fox_k), rows_last(cache_fox_v), cache_fox_logf[0].transpose(0, 2, 1))
    ob_s = _moba_decode(page_table, tokh(qb_s), tokh(kb_s), tokh(vb_s), t5_table[:, :, None],
                        rows_last(cache_moba_k), rows_last(cache_moba_v))
    flat = lambda o: o.reshape(n_s, W_HEADS).astype(BF16)
    x1_s, h2_s, route_s = _merge(xs, flat(oa_s), flat(ob_s), sga_s, sgb_s, *merge_w, tm=n_s)
    y_sample = _moe(x1_s, h2_s, route_s, wg, wu, wd, tm=n_s).reshape(nb, nt, D_MODEL)

    kv_p = lambda a: a.reshape(depth, batch, seq, N_HEADS, HEAD_DIM)
    kv_s = lambda a: a.reshape(depth, nb, nt, N_HEADS, HEAD_DIM)
    return (y_prompt, y_sample,
            kv_p(ka), kv_p(va), lf[:, :N_HEADS].reshape(depth, batch, seq, N_HEADS), kv_p(kb), kv_p(vb),
            kv_s(ka_s), kv_s(va_s), lf_s[:, :N_HEADS].reshape(depth, nb, nt, N_HEADS), kv_s(kb_s), kv_s(vb_s))
```

```python
import functools
import math

import numpy as np
import jax
import jax.numpy as jnp
from jax import lax
from jax.experimental import pallas as pl
from jax.experimental.pallas import tpu as pltpu

F32 = jnp.float32
BF16 = jnp.bfloat16
HIGHEST = lax.Precision.HIGHEST

D_MODEL = 1024
N_HEADS = 8
HEAD_DIM = 64
W_HEADS = N_HEADS * HEAD_DIM
MOBA_BLOCK = 256
MOBA_TOPK = 3
T5_BUCKETS = 32
T5_MAX_DIST = 128
N_GROUPS = 4
EXPERTS_PER_GROUP = 8
N_EXPERTS = N_GROUPS * EXPERTS_PER_GROUP
D_EXPERT = 256
EPS = 1e-6
NEG = -1e30
SCALE = HEAD_DIM ** -0.5
LOG2E = 1.4426950408889634
LANES = 128
ATTN_TILE = 2 * MOBA_BLOCK
VMEM_LIMIT = 56 * 1024 * 1024


def _t5_thresholds():
    max_exact = T5_BUCKETS // 2
    n = np.arange(0, 4 * T5_MAX_DIST)
    nf = np.maximum(n, max_exact).astype(np.float32)
    large = max_exact + (np.log(nf / np.float32(max_exact)) / np.float32(math.log(T5_MAX_DIST / max_exact))
                         * np.float32(T5_BUCKETS - max_exact)).astype(np.int32)
    bucket = np.where(n < max_exact, n, np.minimum(large, T5_BUCKETS - 1))
    return [int(np.argmax(bucket >= j)) for j in range(1, T5_BUCKETS)]


T5_THRESHOLDS = _t5_thresholds()


def _params(*sem):
    return pltpu.CompilerParams(dimension_semantics=sem, vmem_limit_bytes=VMEM_LIMIT)


def _proj_kernel(x_ref, g_ref, w_ref, wf_ref, bf_ref, gains_ref, seg_ref,
                 qa_ref, ka_ref, va_ref, lf_ref, qb_ref, kb_ref, vb_ref, sga_ref, sgb_ref, km_ref):
    x = x_ref[...]
    h = x * lax.rsqrt(jnp.mean(x * x, axis=-1, keepdims=True) + EPS) * g_ref[...]
    hb = h.astype(BF16)
    seg = seg_ref[...]

    def proj(i, width=W_HEADS):
        return jnp.dot(hb, w_ref[:, i:i + width], preferred_element_type=F32)

    def headnorm(z, row):
        ms = jnp.dot((z * z).astype(BF16), seg, preferred_element_type=F32)
        return z * lax.rsqrt(ms + EPS) * gains_ref[row:row + 1, :]

    qa_ref[...] = headnorm(proj(0), 0)
    ka_ref[...] = headnorm(proj(W_HEADS), 1)
    va_ref[...] = proj(2 * W_HEADS)
    qb_ref[...] = headnorm(proj(3 * W_HEADS), 2)
    kb = headnorm(proj(4 * W_HEADS), 3)
    kb_ref[...] = kb
    km_ref[...] = jnp.mean(kb, axis=0, keepdims=True)
    vb_ref[...] = proj(5 * W_HEADS)
    sga_ref[...] = jax.nn.sigmoid(proj(6 * W_HEADS, D_MODEL)).astype(BF16)
    sgb_ref[...] = jax.nn.sigmoid(proj(6 * W_HEADS + D_MODEL, D_MODEL)).astype(BF16)
    f = jnp.dot(hb, wf_ref[...], preferred_element_type=F32) + bf_ref[...]
    lf_ref[...] = jnp.minimum(f, 0.0) - jnp.log1p(jnp.exp(-jnp.abs(f)))


def _project(x2d, tm, g_attn, w_main, w_f, b_f, gains, seg):
    n = x2d.shape[0]
    row = lambda i: (i, 0)
    const = lambda i: (0, 0)
    wide = lambda w, dt: jax.ShapeDtypeStruct((n, w), dt)
    out_shape = (wide(W_HEADS, F32), wide(W_HEADS, F32), wide(W_HEADS, F32), wide(LANES, F32),
                 wide(W_HEADS, F32), wide(W_HEADS, F32), wide(W_HEADS, F32),
                 wide(D_MODEL, BF16), wide(D_MODEL, BF16),
                 jax.ShapeDtypeStruct((n // tm, 1, W_HEADS), F32))
    blk = lambda w: pl.BlockSpec((tm, w), row)
    out_specs = (blk(W_HEADS), blk(W_HEADS), blk(W_HEADS), blk(LANES), blk(W_HEADS), blk(W_HEADS), blk(W_HEADS),
                 blk(D_MODEL), blk(D_MODEL), pl.BlockSpec((None, 1, W_HEADS), lambda i: (i, 0, 0)))
    in_specs = [blk(D_MODEL), pl.BlockSpec((1, D_MODEL), const), pl.BlockSpec(w_main.shape, const),
                pl.BlockSpec(w_f.shape, const), pl.BlockSpec((1, LANES), const),
                pl.BlockSpec(gains.shape, const), pl.BlockSpec(seg.shape, const)]
    return pl.pallas_call(_proj_kernel, grid=(n // tm,), in_specs=in_specs, out_specs=out_specs,
                          out_shape=out_shape, compiler_params=_params("arbitrary"), name="proj")(
        x2d, g_attn, w_main, w_f, b_f, gains, seg)


def _cumsum_kernel(lf_ref, o_ref, carry_ref, *, tc):
    @pl.when(pl.program_id(1) == 0)
    def _():
        carry_ref[...] = jnp.zeros_like(carry_ref)

    r = lax.broadcasted_iota(jnp.int32, (tc, tc), 0)
    c = lax.broadcasted_iota(jnp.int32, (tc, tc), 1)
    tri = jnp.where(c <= r, 1.0, 0.0).astype(F32)
    out = jnp.dot(tri, lf_ref[...], preferred_element_type=F32, precision=HIGHEST) + carry_ref[...]
    o_ref[...] = out
    carry_ref[...] = out[tc - 1:tc, :]


def _cumsum(lf, batch, seq, tc=256):
    nchunk = seq // tc
    spec = pl.BlockSpec((tc, LANES), lambda b, i: (b * nchunk + i, 0))
    return pl.pallas_call(functools.partial(_cumsum_kernel, tc=tc), grid=(batch, nchunk),
                          in_specs=[spec], out_specs=spec,
                          out_shape=jax.ShapeDtypeStruct(lf.shape, F32),
                          scratch_shapes=[pltpu.VMEM((1, LANES), F32)],
                          compiler_params=_params("arbitrary", "arbitrary"), name="logf_cumsum")(lf)


def _online_update(carry, s2, v):
    m, l, acc = carry
    m_new = jnp.maximum(m, jnp.max(s2, axis=-1, keepdims=True))
    alpha = jnp.exp2(m - m_new)
    p = jnp.exp2(s2 - m_new)
    l = alpha * l + jnp.sum(p, axis=-1, keepdims=True)
    acc = alpha * acc + jnp.dot(p.astype(BF16), v, preferred_element_type=F32)
    return m_new, l, acc


def _softmax_init(rows):
    return (jnp.full((rows, 1), NEG, F32), jnp.zeros((rows, 1), F32), jnp.zeros((rows, HEAD_DIM), F32))


def _split3(c):
    def top(x):
        bits = lax.bitcast_convert_type(x, jnp.uint32) & jnp.uint32(0xFFFF0000)
        return lax.bitcast_convert_type(bits, F32)
    hi = top(c)
    mid = top(c - hi)
    lo = c - hi - mid
    return hi.astype(BF16), mid.astype(BF16), lo.astype(BF16)


def _fox_kernel(q_ref, kt_ref, v_ref, o_ref, *, tq):
    qi = pl.program_id(2)
    q = q_ref[...]

    def tile(c):
        k0 = pl.multiple_of(c * tq, tq)
        return jnp.dot(q, kt_ref[:, pl.ds(k0, tq)], preferred_element_type=F32), v_ref[pl.ds(k0, tq), :]

    carry = lax.fori_loop(0, qi, lambda c, carry: _online_update(carry, *tile(c)), _softmax_init(tq))
    s2, v = tile(qi)
    row = lax.broadcasted_iota(jnp.int32, (tq, tq), 0)
    col = lax.broadcasted_iota(jnp.int32, (tq, tq), 1)
    m, l, acc = _online_update(carry, jnp.where(col <= row, s2, NEG), v)
    o_ref[...] = (acc / l).astype(o_ref.dtype)


def _fox_prompt(q_aug, kt_aug, v, tq=ATTN_TILE):
    b, h, s, _ = q_aug.shape
    in_specs = [pl.BlockSpec((None, None, tq, LANES), lambda b, h, i: (b, h, i, 0)),
                pl.BlockSpec((None, None, LANES, s), lambda b, h, i: (b, h, 0, 0)),
                pl.BlockSpec((None, None, s, HEAD_DIM), lambda b, h, i: (b, h, 0, 0))]
    out_specs = pl.BlockSpec((None, None, tq, HEAD_DIM), lambda b, h, i: (b, h, i, 0))
    return pl.pallas_call(functools.partial(_fox_kernel, tq=tq), grid=(b, h, s // tq), in_specs=in_specs,
                          out_specs=out_specs, out_shape=jax.ShapeDtypeStruct((b, h, s, HEAD_DIM), BF16),
                          compiler_params=_params("arbitrary", "arbitrary", "arbitrary"), name="fox_prompt")(
        q_aug, kt_aug, v)


def _fox_operands(qa, ka, va, cum, batch, seq):
    c = (cum * LOG2E).transpose(0, 2, 1)
    pieces = _split3(c)
    ones = jnp.ones_like(pieces[0])
    q = (_heads(qa, batch, seq) * (SCALE * LOG2E)).astype(BF16)
    q_cols = [p[..., None] for p in pieces] + [ones[..., None]] * 3
    q_aug = jnp.concatenate([q] + q_cols + [jnp.zeros((batch, N_HEADS, seq, LANES - HEAD_DIM - 6), BF16)], axis=-1)
    kt = _heads(ka, batch, seq).transpose(0, 1, 3, 2).astype(BF16)
    k_rows = [ones[:, :, None]] * 3 + [-p[:, :, None] for p in pieces]
    kt_aug = jnp.concatenate([kt] + k_rows + [jnp.zeros((batch, N_HEADS, LANES - HEAD_DIM - 6, seq), BF16)], axis=2)
    return q_aug, kt_aug, _heads(va, batch, seq).astype(BF16)


def _t5_select(dist, table_at):
    bias = table_at(0)
    for j, thr in enumerate(T5_THRESHOLDS, start=1):
        bias = jnp.where(dist >= thr, table_at(j), bias)
    return bias


def _bias_kernel(tab_ref, o_ref):
    hh = pl.program_id(0)
    i = lax.broadcasted_iota(jnp.int32, (MOBA_BLOCK, 2 * MOBA_BLOCK), 0)
    j = lax.broadcasted_iota(jnp.int32, (MOBA_BLOCK, 2 * MOBA_BLOCK), 1)
    dist = MOBA_BLOCK + i - j
    o_ref[...] = _t5_select(dist, lambda bkt: tab_ref[bkt, hh])


def _moba_bias(t5_table):
    return pl.pallas_call(_bias_kernel, grid=(N_HEADS,),
                          in_specs=[pl.BlockSpec(memory_space=pltpu.SMEM)],
                          out_specs=pl.BlockSpec((None, MOBA_BLOCK, 2 * MOBA_BLOCK), lambda h: (h, 0, 0)),
                          out_shape=jax.ShapeDtypeStruct((N_HEADS, MOBA_BLOCK, 2 * MOBA_BLOCK), F32),
                          compiler_params=_params("arbitrary"), name="moba_bias")(t5_table)


MOBA_MAX_BLOCKS = 32
SEL_LANE0 = HEAD_DIM
SEL_LANE1 = HEAD_DIM + MOBA_MAX_BLOCKS


def _moba_kernel(tab_ref, qp_ref, q32_ref, kt_ref, v_ref, km_ref, own_ref, prev_ref, prevtile_ref, o_ref, qsel_ref):
    blk, tq = MOBA_BLOCK, 2 * MOBA_BLOCK
    hh, i = pl.program_id(1), pl.program_id(2)
    qp = qp_ref[...]

    nblk = MOBA_MAX_BLOCKS
    gate = lax.dot_general(km_ref[...], q32_ref[...], (((1,), (1,)), ((), ())),
                           preferred_element_type=F32, precision=HIGHEST)
    blk_i = lax.broadcasted_iota(jnp.int32, (nblk, tq), 0)
    blk_f = blk_i.astype(F32)
    own = 2 * i + (lax.broadcasted_iota(jnp.int32, (nblk, tq), 1) >= blk).astype(jnp.int32)
    g = jnp.where(blk_i < own, gate, -jnp.inf)
    chosen = jnp.zeros((nblk, tq), F32)
    for _ in range(MOBA_TOPK):
        mx = jnp.max(g, axis=0, keepdims=True)
        idx = jnp.min(jnp.where(g == mx, blk_f, float(nblk)), axis=0, keepdims=True)
        pick = jnp.logical_and(blk_f == idx, mx > -jnp.inf)
        chosen = jnp.where(pick, 1.0, chosen)
        g = jnp.where(pick, -jnp.inf, g)
    far = jnp.full((1, 1), tab_ref[T5_BUCKETS - 1, hh] * LOG2E, F32)
    far_hi = far.astype(BF16).astype(F32)
    picked = chosen > 0.5
    sel_t = jnp.concatenate([jnp.zeros((SEL_LANE0, tq), F32), jnp.where(picked, far_hi, NEG),
                             jnp.where(picked, far - far_hi, 0.0)], axis=0)
    lane = lax.broadcasted_iota(jnp.int32, (tq, LANES), 1)
    qsel_ref[...] = jnp.where(lane >= SEL_LANE0, sel_t.T.astype(BF16), qp)

    k0 = pl.multiple_of(i * tq, tq)
    kt_d = kt_ref[:, pl.ds(k0, tq)]
    v_d = v_ref[pl.ds(k0, tq), :]
    row = lax.broadcasted_iota(jnp.int32, (blk, blk), 0)
    col = lax.broadcasted_iota(jnp.int32, (blk, blk), 1)
    causal = col <= row
    own_bias = own_ref[...]
    dot = functools.partial(jnp.dot, preferred_element_type=F32)
    s_a = jnp.where(causal, dot(qp[:blk], kt_d[:, :blk]) + own_bias, NEG)
    s_b = dot(qsel_ref[blk:, :], kt_d[:, :blk]) + prev_ref[...]
    s_c = jnp.where(causal, dot(qp[blk:], kt_d[:, blk:]) + own_bias, NEG)
    s_d = jnp.concatenate([jnp.concatenate([s_a, jnp.full((blk, blk), NEG, F32)], axis=1),
                           jnp.concatenate([s_b, s_c], axis=1)], axis=0)
    carry = _online_update(_softmax_init(tq), s_d, v_d)

    def tile(c, carry, near):
        k0 = pl.multiple_of(c * tq, tq)
        s2 = dot(qsel_ref[...], kt_ref[:, pl.ds(k0, tq)])
        if near:
            s2 = s2 + prevtile_ref[...]
        return _online_update(carry, s2, v_ref[pl.ds(k0, tq), :])

    carry = lax.fori_loop(jnp.maximum(i - 1, 0), i, functools.partial(tile, near=True), carry)
    m, l, acc = lax.fori_loop(0, i - 1, functools.partial(tile, near=False), carry)
    o_ref[...] = (acc / l).astype(o_ref.dtype)


def _moba_prompt(t5_table, q_pad, q32, kt_aug, v, km, own_bias, prev_delta, prev_tile):
    b, h, s, _ = q_pad.shape
    blk, tq = MOBA_BLOCK, 2 * MOBA_BLOCK
    per_head = lambda shape: pl.BlockSpec((None,) + shape, lambda b, h, i: (h, 0, 0))
    in_specs = [pl.BlockSpec(memory_space=pltpu.SMEM),
                pl.BlockSpec((None, None, tq, LANES), lambda b, h, i: (b, h, i, 0)),
                pl.BlockSpec((None, None, tq, HEAD_DIM), lambda b, h, i: (b, h, i, 0)),
                pl.BlockSpec((None, None, LANES, s), lambda b, h, i: (b, h, 0, 0)),
                pl.BlockSpec((None, None, s, HEAD_DIM), lambda b, h, i: (b, h, 0, 0)),
                pl.BlockSpec((None, None, MOBA_MAX_BLOCKS, HEAD_DIM), lambda b, h, i: (b, h, 0, 0)),
                per_head((blk, blk)), per_head((blk, blk)), per_head((tq, tq))]
    out_specs = pl.BlockSpec((None, None, tq, HEAD_DIM), lambda b, h, i: (b, h, i, 0))
    return pl.pallas_call(_moba_kernel, grid=(b, h, s // tq), in_specs=in_specs, out_specs=out_specs,
                          out_shape=jax.ShapeDtypeStruct((b, h, s, HEAD_DIM), BF16),
                          scratch_shapes=[pltpu.VMEM((tq, LANES), BF16)],
                          compiler_params=_params("arbitrary", "arbitrary", "arbitrary"), name="moba_prompt")(
        t5_table, q_pad, q32, kt_aug, v, km, own_bias, prev_delta, prev_tile)


def _moba_operands(qb, kb, vb, kmean, bias_tiles, t5_table, batch, seq):
    blk = MOBA_BLOCK
    nb = seq // blk
    assert nb <= MOBA_MAX_BLOCKS
    q32 = _heads(qb, batch, seq)
    q_pad = jnp.pad((q32 * (SCALE * LOG2E)).astype(BF16), ((0, 0), (0, 0), (0, 0), (0, LANES - HEAD_DIM)))
    kt = _heads(kb, batch, seq).transpose(0, 1, 3, 2).astype(BF16)
    member = (jnp.arange(MOBA_MAX_BLOCKS)[:, None] == jnp.arange(seq)[None] // blk).astype(BF16)
    member = jnp.broadcast_to(member, (batch, N_HEADS) + member.shape)
    kt_aug = jnp.concatenate([kt, member, member], axis=2)
    km = kmean.reshape(batch, nb, N_HEADS, HEAD_DIM).transpose(0, 2, 1, 3)
    km = jnp.pad(km, ((0, 0), (0, 0), (0, MOBA_MAX_BLOCKS - nb), (0, 0)))
    far = t5_table[T5_BUCKETS - 1][:, None, None]
    own_bias = bias_tiles[:, :, blk:] * LOG2E
    prev_delta = (bias_tiles[:, :, :blk] - far) * LOG2E
    prev_tile = jnp.zeros((N_HEADS, 2 * blk, 2 * blk), F32).at[:, :blk, blk:].set(prev_delta)
    return q_pad, q32, kt_aug, _heads(vb, batch, seq).astype(BF16), km, own_bias, prev_delta, prev_tile


def _merge_kernel(x_ref, oa_ref, ob_ref, sga_ref, sgb_ref, wbf_ref, wbm_ref, wout_ref, g_ref, wr_ref,
                  x1_ref, h2_ref, route_ref):
    ma = jnp.dot(oa_ref[...], wbf_ref[...], preferred_element_type=F32)
    mb = jnp.dot(ob_ref[...], wbm_ref[...], preferred_element_type=F32)
    mix = sga_ref[...].astype(F32) * ma + sgb_ref[...].astype(F32) * mb
    x1 = x_ref[...] + jnp.dot(mix.astype(BF16), wout_ref[...], preferred_element_type=F32)
    x1_ref[...] = x1
    h2 = x1 * lax.rsqrt(jnp.mean(x1 * x1, axis=-1, keepdims=True) + EPS) * g_ref[...]
    h2_ref[...] = h2.astype(BF16)

    logits = jnp.dot(h2, wr_ref[...], preferred_element_type=F32, precision=HIGHEST)
    lane = lax.broadcasted_iota(jnp.int32, logits.shape, 1)
    big = LANES

    def argmax_low(vals):
        mx = jnp.max(vals, axis=-1, keepdims=True)
        return mx, jnp.min(jnp.where(vals == mx, lane, big), axis=-1, keepdims=True)

    glog = jnp.where(lane < N_GROUPS, logits, -jnp.inf)
    gmax, gsel = argmax_low(glog)
    gprob = 1.0 / jnp.sum(jnp.exp(glog - gmax), axis=-1, keepdims=True)
    first = N_GROUPS + gsel * EXPERTS_PER_GROUP
    in_group = jnp.logical_and(lane >= first, lane < first + EXPERTS_PER_GROUP)
    elog = jnp.where(in_group, logits, -jnp.inf)
    v0, i0 = argmax_low(elog)
    v1, i1 = argmax_low(jnp.where(lane == i0, -jnp.inf, elog))
    e1 = jnp.exp(v1 - v0)
    w0 = gprob / (1.0 + e1)
    w1 = gprob * e1 / (1.0 + e1)
    route = jnp.where(lane == 0, (i0 - N_GROUPS).astype(F32),
                      jnp.where(lane == 1, (i1 - N_GROUPS).astype(F32),
                                jnp.where(lane == 2, w0, jnp.where(lane == 3, w1, 0.0))))
    route_ref[...] = route


def _merge(x2d, oa, ob, sga, sgb, wbf, wbm, wout, g_ffn, w_router, tm):
    n = x2d.shape[0]
    row = lambda i: (i, 0)
    const = lambda i: (0, 0)
    blk = lambda w: pl.BlockSpec((tm, w), row)
    full = lambda a: pl.BlockSpec(a.shape, const)
    in_specs = [blk(D_MODEL), blk(W_HEADS), blk(W_HEADS), blk(D_MODEL), blk(D_MODEL),
                full(wbf), full(wbm), full(wout), full(g_ffn), full(w_router)]
    out_shape = (jax.ShapeDtypeStruct((n, D_MODEL), F32), jax.ShapeDtypeStruct((n, D_MODEL), BF16),
                 jax.ShapeDtypeStruct((n, LANES), F32))
    return pl.pallas_call(_merge_kernel, grid=(n // tm,), in_specs=in_specs,
                          out_specs=(blk(D_MODEL), blk(D_MODEL), blk(LANES)), out_shape=out_shape,
                          compiler_params=_params("arbitrary"), name="merge_router")(
        x2d, oa, ob, sga, sgb, wbf, wbm, wout, g_ffn, w_router)


def _moe_kernel(x1_ref, h2_ref, route_ref, wg_ref, wu_ref, wd_ref, o_ref):
    e = pl.program_id(1)

    @pl.when(e == 0)
    def _():
        o_ref[...] = x1_ref[...]

    h = h2_ref[...]
    a = jax.nn.silu(jnp.dot(h, wg_ref[...], preferred_element_type=F32)) * \
        jnp.dot(h, wu_ref[...], preferred_element_type=F32)
    y = jnp.dot(a.astype(BF16), wd_ref[...], preferred_element_type=F32)
    r = route_ref[...]
    ef = e.astype(F32)
    gate = jnp.where(r[:, 0:1] == ef, r[:, 2:3], 0.0) + jnp.where(r[:, 1:2] == ef, r[:, 3:4], 0.0)
    o_ref[...] += y * gate


def _moe(x1, h2, route, wg, wu, wd, tm):
    n = x1.shape[0]
    row = lambda i, e: (i, 0)
    in_specs = [pl.BlockSpec((tm, D_MODEL), row), pl.BlockSpec((tm, D_MODEL), row), pl.BlockSpec((tm, LANES), row),
                pl.BlockSpec((None, D_MODEL, D_EXPERT), lambda i, e: (e, 0, 0)),
                pl.BlockSpec((None, D_MODEL, D_EXPERT), lambda i, e: (e, 0, 0)),
                pl.BlockSpec((None, D_EXPERT, D_MODEL), lambda i, e: (e, 0, 0))]
    return pl.pallas_call(_moe_kernel, grid=(n // tm, N_EXPERTS), in_specs=in_specs,
                          out_specs=pl.BlockSpec((tm, D_MODEL), row),
                          out_shape=jax.ShapeDtypeStruct((n, D_MODEL), F32),
                          compiler_params=_params("arbitrary", "arbitrary"), name="experts")(
        x1, h2, route, wg, wu, wd)


DECODE_PAGES_PER_STEP = 4


def _page_logits(kt, q3):
    return jnp.sum(kt * q3, axis=1)


def _fox_decode_kernel(pt_ref, q_ref, kn_ref, vn_ref, lfn_ref, *rest, page, group):
    k_refs, v_refs, lf_refs = rest[:group], rest[group:2 * group], rest[2 * group:3 * group]
    o_ref, m_ref, l_ref, acc_ref, carry_ref = rest[3 * group:]
    j = pl.program_id(1)
    q3 = q_ref[...] * SCALE

    @pl.when(j == 0)
    def _():
        m_ref[...] = jnp.sum(q3 * kn_ref[...], axis=1)
        l_ref[...] = jnp.ones_like(l_ref)
        lane = lax.broadcasted_iota(jnp.int32, acc_ref.shape, 2)
        acc_ref[...] = jnp.where(lane == 0, vn_ref[...], 0.0)
        carry_ref[...] = lfn_ref[...]

    r0 = lax.broadcasted_iota(jnp.int32, (page, page), 0)
    r1 = lax.broadcasted_iota(jnp.int32, (page, page), 1)
    later = jnp.where(r0 > r1, 1.0, 0.0).astype(F32)
    for g in range(group):
        lf = lf_refs[g][...]
        decay = jnp.dot(lf, later, preferred_element_type=F32, precision=HIGHEST) + carry_ref[...]
        s = _page_logits(k_refs[g][...], q3) + decay
        m = m_ref[...]
        m_new = jnp.maximum(m, jnp.max(s, axis=-1, keepdims=True))
        alpha = jnp.exp(m - m_new)
        p = jnp.exp(s - m_new)
        l_ref[...] = alpha * l_ref[...] + jnp.sum(p, axis=-1, keepdims=True)
        acc_ref[...] = alpha[:, :, None] * acc_ref[...] + v_refs[g][...] * p[:, None, :]
        m_ref[...] = m_new
        carry_ref[...] = carry_ref[...] + jnp.sum(lf, axis=-1, keepdims=True)

    @pl.when(j == pl.num_programs(1) - 1)
    def _():
        o_ref[...] = jnp.sum(acc_ref[...], axis=-1, keepdims=True) / l_ref[...][:, :, None]


def _fox_decode(page_table, q, kn, vn, lfn, kt_cache, vt_cache, lft_cache):
    nb, n_pages = page_table.shape
    page = kt_cache.shape[-1]
    group = math.gcd(DECODE_PAGES_PER_STEP, n_pages)
    tok4 = lambda b, j, pt: (b, 0, 0, 0)
    tok3 = lambda b, j, pt: (b, 0, 0)

    def paged(g, ndim):
        return lambda b, j, pt: (pt[b, n_pages - 1 - (j * group + g)],) + (0,) * ndim

    col = pl.BlockSpec((None, N_HEADS, HEAD_DIM, 1), tok4)
    kv_specs = [pl.BlockSpec((None, N_HEADS, HEAD_DIM, page), paged(g, 3)) for g in range(group)]
    lf_specs = [pl.BlockSpec((None, N_HEADS, page), paged(g, 2)) for g in range(group)]
    in_specs = [col, col, col, pl.BlockSpec((None, N_HEADS, 1), tok3)] + kv_specs + kv_specs + lf_specs
    grid_spec = pltpu.PrefetchScalarGridSpec(
        num_scalar_prefetch=1, grid=(nb, n_pages // group), in_specs=in_specs, out_specs=col,
        scratch_shapes=[pltpu.VMEM((N_HEADS, 1), F32), pltpu.VMEM((N_HEADS, 1), F32),
                        pltpu.VMEM((N_HEADS, HEAD_DIM, page), F32), pltpu.VMEM((N_HEADS, 1), F32)])
    return pl.pallas_call(functools.partial(_fox_decode_kernel, page=page, group=group), grid_spec=grid_spec,
                          out_shape=jax.ShapeDtypeStruct((nb, N_HEADS, HEAD_DIM, 1), F32),
                          compiler_params=_params("arbitrary", "arbitrary"), name="fox_decode")(
        page_table, q, kn, vn, lfn, *([kt_cache] * group), *([vt_cache] * group), *([lft_cache] * group))


def _moba_score_kernel(pt_ref, q_ref, kn_ref, vn_ref, tab_ref, *rest, page, n_pages, group):
    k_refs = rest[:group]
    p_ref, sel_ref, own_ref, s_ref = rest[group:]
    j = pl.program_id(1)
    ppb = MOBA_BLOCK // page
    nblk = n_pages // ppb
    p_len = n_pages * page
    q3 = q_ref[...]
    for g in range(group):
        s_ref[pl.ds(j * group + g, 1)] = _page_logits(k_refs[g][...], q3)[None]

    @pl.when(j == pl.num_programs(1) - 1)
    def _():
        s = s_ref[...].reshape(nblk, ppb, N_HEADS, page)
        gate = jnp.sum(jnp.sum(s, axis=1), axis=-1, keepdims=True) * (1.0 / MOBA_BLOCK)
        idx = lax.broadcasted_iota(jnp.int32, gate.shape, 0)
        lane = lax.broadcasted_iota(jnp.int32, (N_HEADS, LANES), 1)
        chosen = jnp.zeros(gate.shape, F32)
        picks = jnp.zeros((N_HEADS, LANES), jnp.int32)
        for r in range(min(MOBA_TOPK, nblk)):
            mx = jnp.max(gate, axis=0, keepdims=True)
            first = jnp.min(jnp.where(gate == mx, idx, nblk), axis=0, keepdims=True)
            pick = idx == first
            chosen = jnp.where(pick, 1.0, chosen)
            gate = jnp.where(pick, -jnp.inf, gate)
            picks = jnp.where(lane == r, first[0], picks)
        sel_ref[...] = picks

        logits = s * SCALE + tab_ref[T5_BUCKETS - 1][None, None]
        masked = jnp.where(chosen[:, None] > 0.5, logits, NEG).reshape(n_pages, N_HEADS, page)
        s_ref[...] = masked
        for pg in range(n_pages):
            if p_len - (pg * page + page - 1) < T5_THRESHOLDS[-1]:
                dist = p_len - pg * page - lax.broadcasted_iota(jnp.int32, (1, page), 1)
                bias = _t5_select(dist, lambda bkt: tab_ref[bkt])
                near = s[pg // ppb, pg % ppb] * SCALE + bias
                s_ref[pg] = jnp.where(chosen[pg // ppb] > 0.5, near, NEG)
        masked = s_ref[...]
        s_own = jnp.sum(q3 * kn_ref[...], axis=1) * SCALE + tab_ref[0]
        m = jnp.maximum(jnp.max(jnp.max(masked, axis=0), axis=-1, keepdims=True), s_own)
        p = jnp.exp(masked - m[None])
        p_own = jnp.exp(s_own - m)
        inv = 1.0 / (jnp.sum(jnp.sum(p, axis=0), axis=-1, keepdims=True) + p_own)
        p_ref[...] = p * inv[None]
        own_ref[...] = (p_own * inv)[:, :, None] * vn_ref[...]


def _moba_gather_kernel(vpage_ref, lpage_ref, *rest):
    v_refs, p_refs = rest[:N_HEADS], rest[N_HEADS:2 * N_HEADS]
    own_ref, o_ref, acc_ref = rest[2 * N_HEADS:]
    t = pl.program_id(1)

    @pl.when(t == 0)
    def _():
        acc_ref[...] = jnp.zeros_like(acc_ref)

    for h in range(N_HEADS):
        acc_ref[h] += v_refs[h][...] * p_refs[h][h:h + 1, :]

    @pl.when(t == pl.num_programs(1) - 1)
    def _():
        o_ref[...] = jnp.sum(acc_ref[...], axis=-1, keepdims=True) + own_ref[...]


def _moba_decode(page_table, q, kn, vn, tab3, kt_cache, vt_cache):
    nb, n_pages = page_table.shape
    page = kt_cache.shape[-1]
    ppb = MOBA_BLOCK // page
    nblk = n_pages // ppb
    ntop = min(MOBA_TOPK, nblk)
    group = math.gcd(DECODE_PAGES_PER_STEP, n_pages)
    tok4 = lambda b, j, pt: (b, 0, 0, 0)
    col = pl.BlockSpec((None, N_HEADS, HEAD_DIM, 1), tok4)
    k_specs = [pl.BlockSpec((None, N_HEADS, HEAD_DIM, page),
                            (lambda g: lambda b, j, pt: (pt[b, j * group + g], 0, 0, 0))(g)) for g in range(group)]
    in_specs = [col, col, col, pl.BlockSpec((T5_BUCKETS, N_HEADS, 1), lambda b, j, pt: (0, 0, 0))] + k_specs
    out_specs = (pl.BlockSpec((None, n_pages, N_HEADS, page), tok4),
                 pl.BlockSpec((None, N_HEADS, LANES), lambda b, j, pt: (b, 0, 0)), col)
    out_shape = (jax.ShapeDtypeStruct((nb, n_pages, N_HEADS, page), F32),
                 jax.ShapeDtypeStruct((nb, N_HEADS, LANES), jnp.int32),
                 jax.ShapeDtypeStruct((nb, N_HEADS, HEAD_DIM, 1), F32))
    grid_spec = pltpu.PrefetchScalarGridSpec(
        num_scalar_prefetch=1, grid=(nb, n_pages // group), in_specs=in_specs, out_specs=out_specs,
        scratch_shapes=[pltpu.VMEM((n_pages, N_HEADS, page), F32)])
    probs, picks, own = pl.pallas_call(
        functools.partial(_moba_score_kernel, page=page, n_pages=n_pages, group=group), grid_spec=grid_spec,
        out_shape=out_shape, compiler_params=_params("arbitrary", "arbitrary"), name="moba_score")(
        page_table, q, kn, vn, tab3, *([kt_cache] * group))

    lpage = (picks[:, :, :ntop, None] * ppb + jnp.arange(ppb)[None, None, None]).reshape(nb, N_HEADS, ntop * ppb)
    vpage = jnp.take_along_axis(page_table[:, None, :], lpage, axis=2)
    nstep = ntop * ppb
    lpage, vpage = lpage.reshape(nb, N_HEADS * nstep), vpage.reshape(nb, N_HEADS * nstep)
    v_specs = [pl.BlockSpec((None, None, HEAD_DIM, page),
                            (lambda h: lambda b, t, vp, lp: (vp[b, h * nstep + t], h, 0, 0))(h))
               for h in range(N_HEADS)]
    p_specs = [pl.BlockSpec((None, None, N_HEADS, page),
                            (lambda h: lambda b, t, vp, lp: (b, lp[b, h * nstep + t], 0, 0))(h))
               for h in range(N_HEADS)]
    col2 = pl.BlockSpec((None, N_HEADS, HEAD_DIM, 1), lambda b, t, vp, lp: (b, 0, 0, 0))
    grid_spec = pltpu.PrefetchScalarGridSpec(
        num_scalar_prefetch=2, grid=(nb, nstep), in_specs=v_specs + p_specs + [col2], out_specs=col2,
        scratch_shapes=[pltpu.VMEM((N_HEADS, HEAD_DIM, page), F32)])
    return pl.pallas_call(_moba_gather_kernel, grid_spec=grid_spec,
                          out_shape=jax.ShapeDtypeStruct((nb, N_HEADS, HEAD_DIM, 1), F32),
                          compiler_params=_params("arbitrary", "arbitrary"), name="moba_gather")(
        vpage, lpage, *([vt_cache] * N_HEADS), *([probs] * N_HEADS), own)


def _heads(a, batch, seq):
    return a.reshape(batch, seq, N_HEADS, HEAD_DIM).transpose(0, 2, 1, 3)


def kernel(x_prompt, x_sample, cache_fox_k, cache_fox_v, cache_fox_logf, cache_moba_k, cache_moba_v, page_table,
           g_attn, w_in, b_forget, g_q_fox, g_k_fox, g_q_moba, g_k_moba, t5_table, w_branch_fox, w_branch_moba,
           w_out, g_ffn, w_router_group, w_router_expert, w_e_gate, w_e_up, w_e_down):
    depth = w_in.shape[0]
    assert depth == 1, "single-layer trunk"
    batch, seq, _ = x_prompt.shape
    nb, nt, _ = x_sample.shape
    assert nt == 1 and seq % (2 * MOBA_BLOCK) == 0
    n_p, n_s = batch * seq, nb * nt

    w = w_in[0]
    f0 = 3 * W_HEADS
    w_main = jnp.concatenate([w[:, :f0], w[:, f0 + N_HEADS:]], axis=1).astype(BF16)
    w_f = jnp.pad(w[:, f0:f0 + N_HEADS], ((0, 0), (0, LANES - N_HEADS))).astype(BF16)
    b_f = jnp.pad(b_forget[0], (0, LANES - N_HEADS))[None]
    gains = jnp.stack([jnp.tile(g[0], N_HEADS) for g in (g_q_fox, g_k_fox, g_q_moba, g_k_moba)])
    gains = jnp.pad(gains, ((0, 4), (0, 0)))
    seg = (jnp.arange(W_HEADS)[:, None] // HEAD_DIM == jnp.arange(W_HEADS)[None] // HEAD_DIM)
    seg = (seg.astype(F32) / HEAD_DIM).astype(BF16)
    wbf, wbm, wout = w_branch_fox[0].astype(BF16), w_branch_moba[0].astype(BF16), w_out[0].astype(BF16)
    w_router = jnp.concatenate([w_router_group[0], w_router_expert[0].reshape(D_MODEL, N_EXPERTS)], axis=1)
    w_router = jnp.pad(w_router, ((0, 0), (0, LANES - N_GROUPS - N_EXPERTS)))
    wg, wu, wd = w_e_gate[0].astype(BF16), w_e_up[0].astype(BF16), w_e_down[0].astype(BF16)
    proj_w = (g_attn, w_main, w_f, b_f, gains, seg)

    xp = x_prompt.reshape(n_p, D_MODEL)
    qa, ka, va, lf, qb, kb, vb, sga, sgb, kmean = _project(xp, MOBA_BLOCK, *proj_w)
    cum = _cumsum(lf, batch, seq)[:, :N_HEADS].reshape(batch, seq, N_HEADS)
    oa = _fox_prompt(*_fox_operands(qa, ka, va, cum, batch, seq))
    ob = _moba_prompt(t5_table, *_moba_operands(qb, kb, vb, kmean, _moba_bias(t5_table), t5_table, batch, seq))
    unheads = lambda o: o.transpose(0, 2, 1, 3).reshape(n_p, W_HEADS)
    merge_w = (wbf, wbm, wout, g_ffn, w_router)
    x1, h2, route = _merge(xp, unheads(oa), unheads(ob), sga, sgb, *merge_w, tm=MOBA_BLOCK)
    y_prompt = _moe(x1, h2, route, wg, wu, wd, tm=min(1024, n_p)).reshape(batch, seq, D_MODEL)

    xs = x_sample.reshape(n_s, D_MODEL)
    qa_s, ka_s, va_s, lf_s, qb_s, kb_s, vb_s, sga_s, sgb_s, _ = _project(xs, n_s, *proj_w)
    tokh = lambda a: a.reshape(n_s, N_HEADS, HEAD_DIM, 1)
    rows_last = lambda c: c[0].transpose(0, 2, 3, 1)
    oa_s = _fox_decode(page_table, tokh(qa_s), tokh(ka_s), tokh(va_s), lf_s[:, :N_HEADS, None],
                       rows_last(cache_fox_k), rows_last(cache_fox_v), cache_fox_logf[0].transpose(0, 2, 1))
    ob_s = _moba_decode(page_table, tokh(qb_s), tokh(kb_s), tokh(vb_s), t5_table[:, :, None],
                        rows_last(cache_moba_k), rows_last(cache_moba_v))
    flat = lambda o: o.reshape(n_s, W_HEADS).astype(BF16)
    x1_s, h2_s, route_s = _merge(xs, flat(oa_s), flat(ob_s), sga_s, sgb_s, *merge_w, tm=n_s)
    y_sample = _moe(x1_s, h2_s, route_s, wg, wu, wd, tm=n_s).reshape(nb, nt, D_MODEL)

    kv_p = lambda a: a.reshape(depth, batch, seq, N_HEADS, HEAD_DIM)
    kv_s = lambda a: a.reshape(depth, nb, nt, N_HEADS, HEAD_DIM)
    return (y_prompt, y_sample,
            kv_p(ka), kv_p(va), lf[:, :N_HEADS].reshape(depth, batch, seq, N_HEADS), kv_p(kb), kv_p(vb),
            kv_s(ka_s), kv_s(va_s), lf_s[:, :N_HEADS].reshape(depth, nb, nt, N_HEADS), kv_s(kb_s), kv_s(vb_s))
```

```python
import functools
import math

import numpy as np
import jax
import jax.numpy as jnp
from jax import lax
from jax.experimental import pallas as pl
from jax.experimental.pallas import tpu as pltpu

F32 = jnp.float32
BF16 = jnp.bfloat16
HIGHEST = lax.Precision.HIGHEST

D_MODEL = 1024
N_HEADS = 8
HEAD_DIM = 64
W_HEADS = N_HEADS * HEAD_DIM
MOBA_BLOCK = 256
MOBA_TOPK = 3
T5_BUCKETS = 32
T5_MAX_DIST = 128
N_GROUPS = 4
EXPERTS_PER_GROUP = 8
N_EXPERTS = N_GROUPS * EXPERTS_PER_GROUP
D_EXPERT = 256
EPS = 1e-6
NEG = -1e30
SCALE = HEAD_DIM ** -0.5
LOG2E = 1.4426950408889634
LANES = 128
ATTN_TILE = 2 * MOBA_BLOCK
VMEM_LIMIT = 56 * 1024 * 1024


def _t5_thresholds():
    max_exact = T5_BUCKETS // 2
    n = np.arange(0, 4 * T5_MAX_DIST)
    nf = np.maximum(n, max_exact).astype(np.float32)
    large = max_exact + (np.log(nf / np.float32(max_exact)) / np.float32(math.log(T5_MAX_DIST / max_exact))
                         * np.float32(T5_BUCKETS - max_exact)).astype(np.int32)
    bucket = np.where(n < max_exact, n, np.minimum(large, T5_BUCKETS - 1))
    return [int(np.argmax(bucket >= j)) for j in range(1, T5_BUCKETS)]


T5_THRESHOLDS = _t5_thresholds()


def _params(*sem):
    return pltpu.CompilerParams(dimension_semantics=sem, vmem_limit_bytes=VMEM_LIMIT)


def _proj_body(x_ref, g_ref, w_ref, wf_ref, bf_ref, gains_ref, seg_ref):
    x = x_ref[...]
    h = x * lax.rsqrt(jnp.mean(x * x, axis=-1, keepdims=True) + EPS) * g_ref[...]
    hb = h.astype(BF16)
    seg = seg_ref[...]

    def proj(i, width=W_HEADS):
        return jnp.dot(hb, w_ref[:, i:i + width], preferred_element_type=F32)

    def headnorm(z, row):
        ms = jnp.dot((z * z).astype(BF16), seg, preferred_element_type=F32)
        return z * lax.rsqrt(ms + EPS) * gains_ref[row:row + 1, :]

    f = jnp.dot(hb, wf_ref[...], preferred_element_type=F32) + bf_ref[...]
    return dict(
        qa=headnorm(proj(0), 0), ka=headnorm(proj(W_HEADS), 1), va=proj(2 * W_HEADS),
        qb=headnorm(proj(3 * W_HEADS), 2), kb=headnorm(proj(4 * W_HEADS), 3), vb=proj(5 * W_HEADS),
        sga=jax.nn.sigmoid(proj(6 * W_HEADS, D_MODEL)).astype(BF16),
        sgb=jax.nn.sigmoid(proj(6 * W_HEADS + D_MODEL, D_MODEL)).astype(BF16),
        lf=jnp.minimum(f, 0.0) - jnp.log1p(jnp.exp(-jnp.abs(f))))


def _proj_kernel(x_ref, g_ref, w_ref, wf_ref, bf_ref, gains_ref, seg_ref,
                 qa_ref, ka_ref, va_ref, lf_ref, qb_ref, kb_ref, vb_ref, sga_ref, sgb_ref, km_ref):
    z = _proj_body(x_ref, g_ref, w_ref, wf_ref, bf_ref, gains_ref, seg_ref)
    for name, ref in (("qa", qa_ref), ("ka", ka_ref), ("va", va_ref), ("lf", lf_ref), ("qb", qb_ref),
                      ("kb", kb_ref), ("vb", vb_ref), ("sga", sga_ref), ("sgb", sgb_ref)):
        ref[...] = z[name]
    km_ref[...] = jnp.mean(z["kb"], axis=0, keepdims=True)


def _proj_prompt_kernel(x_ref, g_ref, w_ref, wf_ref, bf_ref, gains_ref, seg_ref,
                        qf_ref, ktf_ref, vf_ref, ktf32_ref, vtf32_ref, lf_ref,
                        qm_ref, qm32_ref, ktm_ref, vm_ref, ktm32_ref, vtm32_ref, km_ref, sga_ref, sgb_ref):
    z = _proj_body(x_ref, g_ref, w_ref, wf_ref, bf_ref, gains_ref, seg_ref)
    tm = x_ref.shape[0]
    lane = lax.broadcasted_iota(jnp.int32, (tm, LANES), 1)
    low = lane < HEAD_DIM
    ones = jnp.where(jnp.logical_and(lane >= HEAD_DIM, lane < HEAD_DIM + 3), 1.0, 0.0)

    def heads_low(a):
        out = []
        for p in range(N_HEADS // 2):
            pair = a[:, p * LANES:(p + 1) * LANES]
            out += [pair, pltpu.roll(pair, HEAD_DIM, 1)]
        return out

    def heads_t(a):
        return a.T.reshape(N_HEADS, HEAD_DIM, tm)

    for hh, q in enumerate(heads_low(z["qa"] * (SCALE * LOG2E))):
        qf_ref[hh] = jnp.where(low, q, ones).astype(BF16)
    for hh, q in enumerate(heads_low(z["qb"])):
        qm32_ref[hh] = q[:, :HEAD_DIM]
        qm_ref[hh] = jnp.where(low, q * (SCALE * LOG2E), 0.0).astype(BF16)
    kat, kbt = heads_t(z["ka"]), heads_t(z["kb"])
    ktf32_ref[...] = kat
    ktf_ref[...] = kat.astype(BF16)
    ktm32_ref[...] = kbt
    ktm_ref[...] = kbt.astype(BF16)
    vtf32_ref[...] = heads_t(z["va"])
    vtm32_ref[...] = heads_t(z["vb"])
    vf_ref[...] = z["va"].astype(BF16)
    vm_ref[...] = z["vb"].astype(BF16)
    km_ref[...] = jnp.mean(z["kb"], axis=0, keepdims=True)
    lf_ref[...] = z["lf"]
    sga_ref[...] = z["sga"]
    sgb_ref[...] = z["sgb"]


def _project_prompt(x2d, batch, seq, tm, g_attn, w_main, w_f, b_f, gains, seg):
    n = x2d.shape[0]
    tpb = seq // tm
    row = lambda i: (i, 0)
    const = lambda i: (0, 0)
    blk = lambda w: pl.BlockSpec((tm, w), row)
    qspec = lambda w: pl.BlockSpec((None, N_HEADS, tm, w), lambda i: (i // tpb, 0, i % tpb, 0))
    tspec = pl.BlockSpec((None, N_HEADS, HEAD_DIM, tm), lambda i: (i // tpb, 0, 0, i % tpb))
    hm = lambda w, dt: jax.ShapeDtypeStruct((batch, N_HEADS, seq, w), dt)
    tr = lambda dt: jax.ShapeDtypeStruct((batch, N_HEADS, HEAD_DIM, seq), dt)
    wide = lambda w, dt: jax.ShapeDtypeStruct((n, w), dt)
    out_shape = (hm(LANES, BF16), tr(BF16), wide(W_HEADS, BF16), tr(F32), tr(F32), wide(LANES, F32),
                 hm(LANES, BF16), hm(HEAD_DIM, F32), tr(BF16), wide(W_HEADS, BF16), tr(F32), tr(F32),
                 jax.ShapeDtypeStruct((n // tm, 1, W_HEADS), F32), wide(D_MODEL, BF16), wide(D_MODEL, BF16))
    out_specs = (qspec(LANES), tspec, blk(W_HEADS), tspec, tspec, blk(LANES),
                 qspec(LANES), qspec(HEAD_DIM), tspec, blk(W_HEADS), tspec, tspec,
                 pl.BlockSpec((None, 1, W_HEADS), lambda i: (i, 0, 0)), blk(D_MODEL), blk(D_MODEL))
    in_specs = [blk(D_MODEL), pl.BlockSpec((1, D_MODEL), const), pl.BlockSpec(w_main.shape, const),
                pl.BlockSpec(w_f.shape, const), pl.BlockSpec((1, LANES), const),
                pl.BlockSpec(gains.shape, const), pl.BlockSpec(seg.shape, const)]
    return pl.pallas_call(_proj_prompt_kernel, grid=(n // tm,), in_specs=in_specs, out_specs=out_specs,
                          out_shape=out_shape, compiler_params=_params("arbitrary"), name="proj_prompt")(
        x2d, g_attn, w_main, w_f, b_f, gains, seg)


def _project(x2d, tm, g_attn, w_main, w_f, b_f, gains, seg):
    n = x2d.shape[0]
    row = lambda i: (i, 0)
    const = lambda i: (0, 0)
    wide = lambda w, dt: jax.ShapeDtypeStruct((n, w), dt)
    out_shape = (wide(W_HEADS, F32), wide(W_HEADS, F32), wide(W_HEADS, F32), wide(LANES, F32),
                 wide(W_HEADS, F32), wide(W_HEADS, F32), wide(W_HEADS, F32),
                 wide(D_MODEL, BF16), wide(D_MODEL, BF16),
                 jax.ShapeDtypeStruct((n // tm, 1, W_HEADS), F32))
    blk = lambda w: pl.BlockSpec((tm, w), row)
    out_specs = (blk(W_HEADS), blk(W_HEADS), blk(W_HEADS), blk(LANES), blk(W_HEADS), blk(W_HEADS), blk(W_HEADS),
                 blk(D_MODEL), blk(D_MODEL), pl.BlockSpec((None, 1, W_HEADS), lambda i: (i, 0, 0)))
    in_specs = [blk(D_MODEL), pl.BlockSpec((1, D_MODEL), const), pl.BlockSpec(w_main.shape, const),
                pl.BlockSpec(w_f.shape, const), pl.BlockSpec((1, LANES), const),
                pl.BlockSpec(gains.shape, const), pl.BlockSpec(seg.shape, const)]
    return pl.pallas_call(_proj_kernel, grid=(n // tm,), in_specs=in_specs, out_specs=out_specs,
                          out_shape=out_shape, compiler_params=_params("arbitrary"), name="proj")(
        x2d, g_attn, w_main, w_f, b_f, gains, seg)


def _cumsum_kernel(lf_ref, o_ref, carry_ref, *, tc):
    @pl.when(pl.program_id(1) == 0)
    def _():
        carry_ref[...] = jnp.zeros_like(carry_ref)

    r = lax.broadcasted_iota(jnp.int32, (tc, tc), 0)
    c = lax.broadcasted_iota(jnp.int32, (tc, tc), 1)
    tri = jnp.where(c <= r, 1.0, 0.0).astype(F32)
    out = jnp.dot(tri, lf_ref[...], preferred_element_type=F32, precision=HIGHEST) + carry_ref[...]
    o_ref[...] = out
    carry_ref[...] = out[tc - 1:tc, :]


def _cumsum(lf, batch, seq, tc=256):
    nchunk = seq // tc
    spec = pl.BlockSpec((tc, LANES), lambda b, i: (b * nchunk + i, 0))
    return pl.pallas_call(functools.partial(_cumsum_kernel, tc=tc), grid=(batch, nchunk),
                          in_specs=[spec], out_specs=spec,
                          out_shape=jax.ShapeDtypeStruct(lf.shape, F32),
                          scratch_shapes=[pltpu.VMEM((1, LANES), F32)],
                          compiler_params=_params("arbitrary", "arbitrary"), name="logf_cumsum")(lf)


def _online_update(carry, s2, v):
    m, l, acc = carry
    m_new = jnp.maximum(m, jnp.max(s2, axis=-1, keepdims=True))
    alpha = jnp.exp2(m - m_new)
    p = jnp.exp2(s2 - m_new)
    l = alpha * l + jnp.sum(p, axis=-1, keepdims=True)
    acc = alpha * acc + jnp.dot(p.astype(BF16), v, preferred_element_type=F32)
    return m_new, l, acc


def _softmax_init(rows, width=LANES):
    return (jnp.full((rows, 1), NEG, F32), jnp.zeros((rows, 1), F32), jnp.zeros((rows, width), F32))


def _pair_output(results):
    (_, l0, a0), (_, l1, a1) = results
    lane = lax.broadcasted_iota(jnp.int32, a0.shape, 1)
    return jnp.where(lane < HEAD_DIM, a0 / l0, a1 / l1)


def _split3(c):
    def top(x):
        bits = lax.bitcast_convert_type(x, jnp.uint32) & jnp.uint32(0xFFFF0000)
        return lax.bitcast_convert_type(bits, F32)
    hi = top(c)
    mid = top(c - hi)
    lo = c - hi - mid
    return hi.astype(BF16), mid.astype(BF16), lo.astype(BF16)


FOX_EXTRA_ROWS = 16


def _fox_kernel(q_ref, cq_ref, kt_ref, kx_ref, v_ref, o_ref, *, tq):
    qi = pl.program_id(2)
    lane = lax.broadcasted_iota(jnp.int32, (tq, LANES), 1)
    row = lax.broadcasted_iota(jnp.int32, (tq, tq), 0)
    col = lax.broadcasted_iota(jnp.int32, (tq, tq), 1)
    pad = jnp.zeros((LANES - HEAD_DIM - FOX_EXTRA_ROWS, tq), BF16)
    results = []
    for hh in range(2):
        q = q_ref[hh].astype(F32)
        for piece in range(3):
            q = jnp.where(lane == HEAD_DIM + 3 + piece, cq_ref[hh, :, piece:piece + 1], q)
        q = q.astype(BF16)

        def tile(c, q=q, hh=hh):
            k0 = pl.multiple_of(c * tq, tq)
            kt = jnp.concatenate([kt_ref[hh, :, pl.ds(k0, tq)], kx_ref[hh, :, pl.ds(k0, tq)], pad], axis=0)
            return jnp.dot(q, kt, preferred_element_type=F32), v_ref[pl.ds(k0, tq), :]

        carry = lax.fori_loop(0, qi, lambda c, carry, tile=tile: _online_update(carry, *tile(c)), _softmax_init(tq))
        s2, v = tile(qi)
        results.append(_online_update(carry, jnp.where(col <= row, s2, NEG), v))
    o_ref[...] = _pair_output(results).astype(o_ref.dtype)


def _fox_prompt(q_aug, cq3, kt, kx, v, tq=ATTN_TILE):
    b, h, s, _ = q_aug.shape
    pair = lambda shape, imap: pl.BlockSpec((None, 2) + shape, imap)
    in_specs = [pair((tq, LANES), lambda b, p, i: (b, p, i, 0)),
                pair((tq, 8), lambda b, p, i: (b, p, i, 0)),
                pair((HEAD_DIM, s), lambda b, p, i: (b, p, 0, 0)),
                pair((FOX_EXTRA_ROWS, s), lambda b, p, i: (b, p, 0, 0)),
                pl.BlockSpec((None, s, LANES), lambda b, p, i: (b, 0, p))]
    out_specs = pl.BlockSpec((None, tq, LANES), lambda b, p, i: (b, i, p))
    return pl.pallas_call(functools.partial(_fox_kernel, tq=tq), grid=(b, h // 2, s // tq), in_specs=in_specs,
                          out_specs=out_specs, out_shape=jax.ShapeDtypeStruct((b, s, h * HEAD_DIM), BF16),
                          compiler_params=_params("arbitrary", "arbitrary", "arbitrary"), name="fox_prompt")(
        q_aug, cq3, kt, kx, v)


def _fox_decay_operands(cum):
    batch, seq, _ = cum.shape
    pieces = [p.astype(F32) for p in _split3((cum * LOG2E).transpose(0, 2, 1))]
    cq3 = jnp.stack(pieces + [jnp.zeros_like(pieces[0])] * 5, axis=-1)
    ones = jnp.ones_like(pieces[0])
    zero = jnp.zeros_like(pieces[0])
    kx = jnp.stack([-p for p in pieces] + [ones] * 3 + [zero] * (FOX_EXTRA_ROWS - 6), axis=2).astype(BF16)
    return cq3, kx


def _t5_select(dist, table_at):
    bias = table_at(0)
    for j, thr in enumerate(T5_THRESHOLDS, start=1):
        bias = jnp.where(dist >= thr, table_at(j), bias)
    return bias


def _bias_kernel(tab_ref, o_ref):
    hh = pl.program_id(0)
    i = lax.broadcasted_iota(jnp.int32, (MOBA_BLOCK, 2 * MOBA_BLOCK), 0)
    j = lax.broadcasted_iota(jnp.int32, (MOBA_BLOCK, 2 * MOBA_BLOCK), 1)
    dist = MOBA_BLOCK + i - j
    o_ref[...] = _t5_select(dist, lambda bkt: tab_ref[bkt, hh])


def _moba_bias(t5_table):
    return pl.pallas_call(_bias_kernel, grid=(N_HEADS,),
                          in_specs=[pl.BlockSpec(memory_space=pltpu.SMEM)],
                          out_specs=pl.BlockSpec((None, MOBA_BLOCK, 2 * MOBA_BLOCK), lambda h: (h, 0, 0)),
                          out_shape=jax.ShapeDtypeStruct((N_HEADS, MOBA_BLOCK, 2 * MOBA_BLOCK), F32),
                          compiler_params=_params("arbitrary"), name="moba_bias")(t5_table)


MOBA_MAX_BLOCKS = 32
SEL_LANE0 = HEAD_DIM
SEL_LANE1 = HEAD_DIM + MOBA_MAX_BLOCKS


def _moba_kernel(tab_ref, qp_ref, q32_ref, kt_ref, member_ref, v_ref, km_ref, own_ref, prev_ref, prevtile_ref,
                 o_ref, qsel_ref):
    blk, tq = MOBA_BLOCK, 2 * MOBA_BLOCK
    pair, i = pl.program_id(1), pl.program_id(2)
    nblk = MOBA_MAX_BLOCKS
    dot = functools.partial(jnp.dot, preferred_element_type=F32)
    blk_i = lax.broadcasted_iota(jnp.int32, (nblk, tq), 0)
    blk_f = blk_i.astype(F32)
    own = 2 * i + (lax.broadcasted_iota(jnp.int32, (nblk, tq), 1) >= blk).astype(jnp.int32)
    lane = lax.broadcasted_iota(jnp.int32, (tq, LANES), 1)
    causal = lax.broadcasted_iota(jnp.int32, (blk, blk), 1) <= lax.broadcasted_iota(jnp.int32, (blk, blk), 0)
    k0 = pl.multiple_of(i * tq, tq)
    v_d = v_ref[pl.ds(k0, tq), :]
    results = []
    for hh in range(2):
        qp = qp_ref[hh]
        gate = lax.dot_general(km_ref[hh], q32_ref[hh], (((1,), (1,)), ((), ())),
                               preferred_element_type=F32, precision=HIGHEST)
        g = jnp.where(blk_i < own, gate, -jnp.inf)
        chosen = jnp.zeros((nblk, tq), F32)
        for _ in range(MOBA_TOPK):
            mx = jnp.max(g, axis=0, keepdims=True)
            idx = jnp.min(jnp.where(g == mx, blk_f, float(nblk)), axis=0, keepdims=True)
            pick = jnp.logical_and(blk_f == idx, mx > -jnp.inf)
            chosen = jnp.where(pick, 1.0, chosen)
            g = jnp.where(pick, -jnp.inf, g)
        far = jnp.full((1, 1), tab_ref[T5_BUCKETS - 1, 2 * pair + hh] * LOG2E, F32)
        far_hi = far.astype(BF16).astype(F32)
        picked = chosen > 0.5
        sel_t = jnp.concatenate([jnp.zeros((SEL_LANE0, tq), F32), jnp.where(picked, far_hi, NEG),
                                 jnp.where(picked, far - far_hi, 0.0)], axis=0)
        qsel_ref[hh] = jnp.where(lane >= SEL_LANE0, sel_t.T.astype(BF16), qp)

        def keys(k0, hh=hh):
            return jnp.concatenate([kt_ref[hh, :, pl.ds(k0, tq)], member_ref[:, pl.ds(k0, tq)]], axis=0)

        kt_d = keys(k0)
        own_bias = own_ref[hh]
        s_a = jnp.where(causal, dot(qp[:blk], kt_d[:, :blk]) + own_bias, NEG)
        s_b = dot(qsel_ref[hh, blk:, :], kt_d[:, :blk]) + prev_ref[hh]
        s_c = jnp.where(causal, dot(qp[blk:], kt_d[:, blk:]) + own_bias, NEG)
        s_d = jnp.concatenate([jnp.concatenate([s_a, jnp.full((blk, blk), NEG, F32)], axis=1),
                               jnp.concatenate([s_b, s_c], axis=1)], axis=0)
        carry = _online_update(_softmax_init(tq), s_d, v_d)

        def tile(c, carry, near, hh=hh, keys=keys):
            k0 = pl.multiple_of(c * tq, tq)
            s2 = dot(qsel_ref[hh], keys(k0))
            if near:
                s2 = s2 + prevtile_ref[hh]
            return _online_update(carry, s2, v_ref[pl.ds(k0, tq), :])

        carry = lax.fori_loop(jnp.maximum(i - 1, 0), i, functools.partial(tile, near=True), carry)
        results.append(lax.fori_loop(0, jnp.maximum(i - 1, 0), functools.partial(tile, near=False), carry))
    o_ref[...] = _pair_output(results).astype(o_ref.dtype)


def _moba_prompt(t5_table, q_pad, q32, kt, member, v, km, own_bias, prev_delta, prev_tile):
    b, h, s, _ = q_pad.shape
    blk, tq = MOBA_BLOCK, 2 * MOBA_BLOCK
    pair = lambda shape, imap: pl.BlockSpec((None, 2) + shape, imap)
    per_pair = lambda shape: pl.BlockSpec((2,) + shape, lambda b, p, i: (p, 0, 0))
    in_specs = [pl.BlockSpec(memory_space=pltpu.SMEM),
                pair((tq, LANES), lambda b, p, i: (b, p, i, 0)),
                pair((tq, HEAD_DIM), lambda b, p, i: (b, p, i, 0)),
                pair((HEAD_DIM, s), lambda b, p, i: (b, p, 0, 0)),
                pl.BlockSpec((2 * MOBA_MAX_BLOCKS, s), lambda b, p, i: (0, 0)),
                pl.BlockSpec((None, s, LANES), lambda b, p, i: (b, 0, p)),
                pair((MOBA_MAX_BLOCKS, HEAD_DIM), lambda b, p, i: (b, p, 0, 0)),
                per_pair((blk, blk)), per_pair((blk, blk)), per_pair((tq, tq))]
    out_specs = pl.BlockSpec((None, tq, LANES), lambda b, p, i: (b, i, p))
    return pl.pallas_call(_moba_kernel, grid=(b, h // 2, s // tq), in_specs=in_specs, out_specs=out_specs,
                          out_shape=jax.ShapeDtypeStruct((b, s, h * HEAD_DIM), BF16),
                          scratch_shapes=[pltpu.VMEM((2, tq, LANES), BF16)],
                          compiler_params=_params("arbitrary", "arbitrary", "arbitrary"), name="moba_prompt")(
        t5_table, q_pad, q32, kt, member, v, km, own_bias, prev_delta, prev_tile)


def _moba_side_operands(kmean, bias_tiles, t5_table, batch, seq):
    blk = MOBA_BLOCK
    nb = seq // blk
    assert nb <= MOBA_MAX_BLOCKS
    member = (jnp.arange(MOBA_MAX_BLOCKS)[:, None] == jnp.arange(seq)[None] // blk).astype(BF16)
    member = jnp.concatenate([member, member], axis=0)
    km = kmean.reshape(batch, nb, N_HEADS, HEAD_DIM).transpose(0, 2, 1, 3)
    km = jnp.pad(km, ((0, 0), (0, 0), (0, MOBA_MAX_BLOCKS - nb), (0, 0)))
    far = t5_table[T5_BUCKETS - 1][:, None, None]
    own_bias = bias_tiles[:, :, blk:] * LOG2E
    prev_delta = (bias_tiles[:, :, :blk] - far) * LOG2E
    prev_tile = jnp.zeros((N_HEADS, 2 * blk, 2 * blk), F32).at[:, :blk, blk:].set(prev_delta)
    return member, km, own_bias, prev_delta, prev_tile


def _merge_kernel(x_ref, oa_ref, ob_ref, sga_ref, sgb_ref, wbf_ref, wbm_ref, wout_ref, g_ref, wr_ref,
                  x1_ref, h2_ref, route_ref):
    ma = jnp.dot(oa_ref[...], wbf_ref[...], preferred_element_type=F32)
    mb = jnp.dot(ob_ref[...], wbm_ref[...], preferred_element_type=F32)
    mix = sga_ref[...].astype(F32) * ma + sgb_ref[...].astype(F32) * mb
    x1 = x_ref[...] + jnp.dot(mix.astype(BF16), wout_ref[...], preferred_element_type=F32)
    x1_ref[...] = x1
    h2 = x1 * lax.rsqrt(jnp.mean(x1 * x1, axis=-1, keepdims=True) + EPS) * g_ref[...]
    h2_ref[...] = h2.astype(BF16)

    logits = jnp.dot(h2, wr_ref[...], preferred_element_type=F32, precision=HIGHEST)
    lane = lax.broadcasted_iota(jnp.int32, logits.shape, 1)
    big = LANES

    def argmax_low(vals):
        mx = jnp.max(vals, axis=-1, keepdims=True)
        return mx, jnp.min(jnp.where(vals == mx, lane, big), axis=-1, keepdims=True)

    glog = jnp.where(lane < N_GROUPS, logits, -jnp.inf)
    gmax, gsel = argmax_low(glog)
    gprob = 1.0 / jnp.sum(jnp.exp(glog - gmax), axis=-1, keepdims=True)
    first = N_GROUPS + gsel * EXPERTS_PER_GROUP
    in_group = jnp.logical_and(lane >= first, lane < first + EXPERTS_PER_GROUP)
    elog = jnp.where(in_group, logits, -jnp.inf)
    v0, i0 = argmax_low(elog)
    v1, i1 = argmax_low(jnp.where(lane == i0, -jnp.inf, elog))
    e1 = jnp.exp(v1 - v0)
    w0 = gprob / (1.0 + e1)
    w1 = gprob * e1 / (1.0 + e1)
    route = jnp.where(lane == 0, (i0 - N_GROUPS).astype(F32),
                      jnp.where(lane == 1, (i1 - N_GROUPS).astype(F32),
                                jnp.where(lane == 2, w0, jnp.where(lane == 3, w1, 0.0))))
    route_ref[...] = route


def _merge(x2d, oa, ob, sga, sgb, wbf, wbm, wout, g_ffn, w_router, tm):
    n = x2d.shape[0]
    row = lambda i: (i, 0)
    const = lambda i: (0, 0)
    blk = lambda w: pl.BlockSpec((tm, w), row)
    full = lambda a: pl.BlockSpec(a.shape, const)
    in_specs = [blk(D_MODEL), blk(W_HEADS), blk(W_HEADS), blk(D_MODEL), blk(D_MODEL),
                full(wbf), full(wbm), full(wout), full(g_ffn), full(w_router)]
    out_shape = (jax.ShapeDtypeStruct((n, D_MODEL), F32), jax.ShapeDtypeStruct((n, D_MODEL), BF16),
                 jax.ShapeDtypeStruct((n, LANES), F32))
    return pl.pallas_call(_merge_kernel, grid=(n // tm,), in_specs=in_specs,
                          out_specs=(blk(D_MODEL), blk(D_MODEL), blk(LANES)), out_shape=out_shape,
                          compiler_params=_params("arbitrary"), name="merge_router")(
        x2d, oa, ob, sga, sgb, wbf, wbm, wout, g_ffn, w_router)


def _moe_kernel(x1_ref, h2_ref, route_ref, wg_ref, wu_ref, wd_ref, o_ref):
    e = pl.program_id(1)

    @pl.when(e == 0)
    def _():
        o_ref[...] = x1_ref[...]

    h = h2_ref[...]
    a = jax.nn.silu(jnp.dot(h, wg_ref[...], preferred_element_type=F32)) * \
        jnp.dot(h, wu_ref[...], preferred_element_type=F32)
    y = jnp.dot(a.astype(BF16), wd_ref[...], preferred_element_type=F32)
    r = route_ref[...]
    ef = e.astype(F32)
    gate = jnp.where(r[:, 0:1] == ef, r[:, 2:3], 0.0) + jnp.where(r[:, 1:2] == ef, r[:, 3:4], 0.0)
    o_ref[...] += y * gate


def _moe(x1, h2, route, wg, wu, wd, tm):
    n = x1.shape[0]
    row = lambda i, e: (i, 0)
    in_specs = [pl.BlockSpec((tm, D_MODEL), row), pl.BlockSpec((tm, D_MODEL), row), pl.BlockSpec((tm, LANES), row),
                pl.BlockSpec((None, D_MODEL, D_EXPERT), lambda i, e: (e, 0, 0)),
                pl.BlockSpec((None, D_MODEL, D_EXPERT), lambda i, e: (e, 0, 0)),
                pl.BlockSpec((None, D_EXPERT, D_MODEL), lambda i, e: (e, 0, 0))]
    return pl.pallas_call(_moe_kernel, grid=(n // tm, N_EXPERTS), in_specs=in_specs,
                          out_specs=pl.BlockSpec((tm, D_MODEL), row),
                          out_shape=jax.ShapeDtypeStruct((n, D_MODEL), F32),
                          compiler_params=_params("arbitrary", "arbitrary"), name="experts")(
        x1, h2, route, wg, wu, wd)


DECODE_PAGES_PER_STEP = 4


def _page_logits(kt, q3):
    return jnp.sum(kt * q3, axis=1)


def _fox_decode_kernel(pt_ref, q_ref, kn_ref, vn_ref, lfn_ref, *rest, page, group):
    k_refs, v_refs, lf_refs = rest[:group], rest[group:2 * group], rest[2 * group:3 * group]
    o_ref, m_ref, l_ref, acc_ref, carry_ref = rest[3 * group:]
    j = pl.program_id(1)
    q3 = q_ref[...] * SCALE

    @pl.when(j == 0)
    def _():
        m_ref[...] = jnp.sum(q3 * kn_ref[...], axis=1)
        l_ref[...] = jnp.ones_like(l_ref)
        lane = lax.broadcasted_iota(jnp.int32, acc_ref.shape, 2)
        acc_ref[...] = jnp.where(lane == 0, vn_ref[...], 0.0)
        carry_ref[...] = lfn_ref[...]

    r0 = lax.broadcasted_iota(jnp.int32, (page, page), 0)
    r1 = lax.broadcasted_iota(jnp.int32, (page, page), 1)
    later = jnp.where(r0 > r1, 1.0, 0.0).astype(F32)
    carry = carry_ref[...]
    logits = []
    for g in range(group):
        lf = lf_refs[g][...]
        decay = jnp.dot(lf, later, preferred_element_type=F32, precision=HIGHEST) + carry
        logits.append(_page_logits(k_refs[g][...], q3) + decay)
        carry = carry + jnp.sum(lf, axis=-1, keepdims=True)
    carry_ref[...] = carry
    m = m_ref[...]
    m_new = m
    for s in logits:
        m_new = jnp.maximum(m_new, jnp.max(s, axis=-1, keepdims=True))
    alpha = jnp.exp(m - m_new)
    l = alpha * l_ref[...]
    acc = alpha[:, :, None] * acc_ref[...]
    for g, s in enumerate(logits):
        p = jnp.exp(s - m_new)
        l = l + jnp.sum(p, axis=-1, keepdims=True)
        acc = acc + v_refs[g][...] * p[:, None, :]
    l_ref[...] = l
    acc_ref[...] = acc
    m_ref[...] = m_new

    @pl.when(j == pl.num_programs(1) - 1)
    def _():
        o_ref[...] = jnp.sum(acc_ref[...], axis=-1, keepdims=True) / l_ref[...][:, :, None]


def _fox_decode(page_table, q, kn, vn, lfn, kt_cache, vt_cache, lft_cache):
    nb, n_pages = page_table.shape
    page = kt_cache.shape[-1]
    group = math.gcd(DECODE_PAGES_PER_STEP, n_pages)
    tok4 = lambda b, j, pt: (b, 0, 0, 0)
    tok3 = lambda b, j, pt: (b, 0, 0)

    def paged(g, ndim):
        return lambda b, j, pt: (pt[b, n_pages - 1 - (j * group + g)],) + (0,) * ndim

    col = pl.BlockSpec((None, N_HEADS, HEAD_DIM, 1), tok4)
    kv_specs = [pl.BlockSpec((None, N_HEADS, HEAD_DIM, page), paged(g, 3)) for g in range(group)]
    lf_specs = [pl.BlockSpec((None, N_HEADS, page), paged(g, 2)) for g in range(group)]
    in_specs = [col, col, col, pl.BlockSpec((None, N_HEADS, 1), tok3)] + kv_specs + kv_specs + lf_specs
    grid_spec = pltpu.PrefetchScalarGridSpec(
        num_scalar_prefetch=1, grid=(nb, n_pages // group), in_specs=in_specs, out_specs=col,
        scratch_shapes=[pltpu.VMEM((N_HEADS, 1), F32), pltpu.VMEM((N_HEADS, 1), F32),
                        pltpu.VMEM((N_HEADS, HEAD_DIM, page), F32), pltpu.VMEM((N_HEADS, 1), F32)])
    return pl.pallas_call(functools.partial(_fox_decode_kernel, page=page, group=group), grid_spec=grid_spec,
                          out_shape=jax.ShapeDtypeStruct((nb, N_HEADS, HEAD_DIM, 1), F32),
                          compiler_params=_params("arbitrary", "arbitrary"), name="fox_decode")(
        page_table, q, kn, vn, lfn, *([kt_cache] * group), *([vt_cache] * group), *([lft_cache] * group))


def _moba_score_kernel(pt_ref, q_ref, kn_ref, vn_ref, tab_ref, *rest, page, n_pages, group):
    k_refs = rest[:group]
    p_ref, sel_ref, own_ref, s_ref = rest[group:]
    j = pl.program_id(1)
    ppb = MOBA_BLOCK // page
    nblk = n_pages // ppb
    p_len = n_pages * page
    q3 = q_ref[...]
    for g in range(group):
        s_ref[pl.ds(j * group + g, 1)] = _page_logits(k_refs[g][...], q3)[None]

    @pl.when(j == pl.num_programs(1) - 1)
    def _():
        s = s_ref[...].reshape(nblk, ppb, N_HEADS, page)
        gate = jnp.sum(jnp.sum(s, axis=1), axis=-1, keepdims=True) * (1.0 / MOBA_BLOCK)
        idx = lax.broadcasted_iota(jnp.int32, gate.shape, 0)
        lane = lax.broadcasted_iota(jnp.int32, (N_HEADS, LANES), 1)
        chosen = jnp.zeros(gate.shape, F32)
        picks = jnp.zeros((N_HEADS, LANES), jnp.int32)
        for r in range(min(MOBA_TOPK, nblk)):
            mx = jnp.max(gate, axis=0, keepdims=True)
            first = jnp.min(jnp.where(gate == mx, idx, nblk), axis=0, keepdims=True)
            pick = idx == first
            chosen = jnp.where(pick, 1.0, chosen)
            gate = jnp.where(pick, -jnp.inf, gate)
            picks = jnp.where(lane == r, first[0], picks)
        sel_ref[...] = picks

        logits = s * SCALE + tab_ref[T5_BUCKETS - 1][None, None]
        masked = jnp.where(chosen[:, None] > 0.5, logits, NEG).reshape(n_pages, N_HEADS, page)
        s_ref[...] = masked
        for pg in range(n_pages):
            if p_len - (pg * page + page - 1) < T5_THRESHOLDS[-1]:
                dist = p_len - pg * page - lax.broadcasted_iota(jnp.int32, (1, page), 1)
                bias = _t5_select(dist, lambda bkt: tab_ref[bkt])
                near = s[pg // ppb, pg % ppb] * SCALE + bias
                s_ref[pg] = jnp.where(chosen[pg // ppb] > 0.5, near, NEG)
        masked = s_ref[...]
        s_own = jnp.sum(q3 * kn_ref[...], axis=1) * SCALE + tab_ref[0]
        m = jnp.maximum(jnp.max(jnp.max(masked, axis=0), axis=-1, keepdims=True), s_own)
        p = jnp.exp(masked - m[None])
        p_own = jnp.exp(s_own - m)
        inv = 1.0 / (jnp.sum(jnp.sum(p, axis=0), axis=-1, keepdims=True) + p_own)
        p_ref[...] = p * inv[None]
        own_ref[...] = (p_own * inv)[:, :, None] * vn_ref[...]


def _moba_gather_kernel(vpage_ref, lpage_ref, *rest):
    v_refs, p_refs = rest[:N_HEADS], rest[N_HEADS:2 * N_HEADS]
    own_ref, o_ref, acc_ref = rest[2 * N_HEADS:]
    t = pl.program_id(1)

    @pl.when(t == 0)
    def _():
        acc_ref[...] = jnp.zeros_like(acc_ref)

    for h in range(N_HEADS):
        acc_ref[h] += v_refs[h][...] * p_refs[h][h:h + 1, :]

    @pl.when(t == pl.num_programs(1) - 1)
    def _():
        o_ref[...] = jnp.sum(acc_ref[...], axis=-1, keepdims=True) + own_ref[...]


def _moba_decode(page_table, q, kn, vn, tab3, kt_cache, vt_cache):
    nb, n_pages = page_table.shape
    page = kt_cache.shape[-1]
    ppb = MOBA_BLOCK // page
    nblk = n_pages // ppb
    ntop = min(MOBA_TOPK, nblk)
    group = math.gcd(DECODE_PAGES_PER_STEP, n_pages)
    tok4 = lambda b, j, pt: (b, 0, 0, 0)
    col = pl.BlockSpec((None, N_HEADS, HEAD_DIM, 1), tok4)
    k_specs = [pl.BlockSpec((None, N_HEADS, HEAD_DIM, page),
                            (lambda g: lambda b, j, pt: (pt[b, j * group + g], 0, 0, 0))(g)) for g in range(group)]
    in_specs = [col, col, col, pl.BlockSpec((T5_BUCKETS, N_HEADS, 1), lambda b, j, pt: (0, 0, 0))] + k_specs
    out_specs = (pl.BlockSpec((None, n_pages, N_HEADS, page), tok4),
                 pl.BlockSpec((None, N_HEADS, LANES), lambda b, j, pt: (b, 0, 0)), col)
    out_shape = (jax.ShapeDtypeStruct((nb, n_pages, N_HEADS, page), F32),
                 jax.ShapeDtypeStruct((nb, N_HEADS, LANES), jnp.int32),
                 jax.ShapeDtypeStruct((nb, N_HEADS, HEAD_DIM, 1), F32))
    grid_spec = pltpu.PrefetchScalarGridSpec(
        num_scalar_prefetch=1, grid=(nb, n_pages // group), in_specs=in_specs, out_specs=out_specs,
        scratch_shapes=[pltpu.VMEM((n_pages, N_HEADS, page), F32)])
    probs, picks, own = pl.pallas_call(
        functools.partial(_moba_score_kernel, page=page, n_pages=n_pages, group=group), grid_spec=grid_spec,
        out_shape=out_shape, compiler_params=_params("arbitrary", "arbitrary"), name="moba_score")(
        page_table, q, kn, vn, tab3, *([kt_cache] * group))

    lpage = (picks[:, :, :ntop, None] * ppb + jnp.arange(ppb)[None, None, None]).reshape(nb, N_HEADS, ntop * ppb)
    vpage = jnp.take_along_axis(page_table[:, None, :], lpage, axis=2)
    nstep = ntop * ppb
    lpage, vpage = lpage.reshape(nb, N_HEADS * nstep), vpage.reshape(nb, N_HEADS * nstep)
    v_specs = [pl.BlockSpec((None, None, HEAD_DIM, page),
                            (lambda h: lambda b, t, vp, lp: (vp[b, h * nstep + t], h, 0, 0))(h))
               for h in range(N_HEADS)]
    p_specs = [pl.BlockSpec((None, None, N_HEADS, page),
                            (lambda h: lambda b, t, vp, lp: (b, lp[b, h * nstep + t], 0, 0))(h))
               for h in range(N_HEADS)]
    col2 = pl.BlockSpec((None, N_HEADS, HEAD_DIM, 1), lambda b, t, vp, lp: (b, 0, 0, 0))
    grid_spec = pltpu.PrefetchScalarGridSpec(
        num_scalar_prefetch=2, grid=(nb, nstep), in_specs=v_specs + p_specs + [col2], out_specs=col2,
        scratch_shapes=[pltpu.VMEM((N_HEADS, HEAD_DIM, page), F32)])
    return pl.pallas_call(_moba_gather_kernel, grid_spec=grid_spec,
                          out_shape=jax.ShapeDtypeStruct((nb, N_HEADS, HEAD_DIM, 1), F32),
                          compiler_params=_params("arbitrary", "arbitrary"), name="moba_gather")(
        vpage, lpage, *([vt_cache] * N_HEADS), *([probs] * N_HEADS), own)


def _heads(a, batch, seq):
    return a.reshape(batch, seq, N_HEADS, HEAD_DIM).transpose(0, 2, 1, 3)


def kernel(x_prompt, x_sample, cache_fox_k, cache_fox_v, cache_fox_logf, cache_moba_k, cache_moba_v, page_table,
           g_attn, w_in, b_forget, g_q_fox, g_k_fox, g_q_moba, g_k_moba, t5_table, w_branch_fox, w_branch_moba,
           w_out, g_ffn, w_router_group, w_router_expert, w_e_gate, w_e_up, w_e_down):
    depth = w_in.shape[0]
    assert depth == 1, "single-layer trunk"
    batch, seq, _ = x_prompt.shape
    nb, nt, _ = x_sample.shape
    assert nt == 1 and seq % (2 * MOBA_BLOCK) == 0
    n_p, n_s = batch * seq, nb * nt

    w = w_in[0]
    f0 = 3 * W_HEADS
    w_main = jnp.concatenate([w[:, :f0], w[:, f0 + N_HEADS:]], axis=1).astype(BF16)
    w_f = jnp.pad(w[:, f0:f0 + N_HEADS], ((0, 0), (0, LANES - N_HEADS))).astype(BF16)
    b_f = jnp.pad(b_forget[0], (0, LANES - N_HEADS))[None]
    gains = jnp.stack([jnp.tile(g[0], N_HEADS) for g in (g_q_fox, g_k_fox, g_q_moba, g_k_moba)])
    gains = jnp.pad(gains, ((0, 4), (0, 0)))
    seg = (jnp.arange(W_HEADS)[:, None] // HEAD_DIM == jnp.arange(W_HEADS)[None] // HEAD_DIM)
    seg = (seg.astype(F32) / HEAD_DIM).astype(BF16)
    wbf, wbm, wout = w_branch_fox[0].astype(BF16), w_branch_moba[0].astype(BF16), w_out[0].astype(BF16)
    w_router = jnp.concatenate([w_router_group[0], w_router_expert[0].reshape(D_MODEL, N_EXPERTS)], axis=1)
    w_router = jnp.pad(w_router, ((0, 0), (0, LANES - N_GROUPS - N_EXPERTS)))
    wg, wu, wd = w_e_gate[0].astype(BF16), w_e_up[0].astype(BF16), w_e_down[0].astype(BF16)
    proj_w = (g_attn, w_main, w_f, b_f, gains, seg)

    xp = x_prompt.reshape(n_p, D_MODEL)
    (q_f, kt_f, v_f, kt_f32, vt_f32, lf, q_m, q32_m, kt_m, v_m, kt_m32, vt_m32, kmean, sga, sgb) = \
        _project_prompt(xp, batch, seq, MOBA_BLOCK, *proj_w)
    cum = _cumsum(lf, batch, seq)[:, :N_HEADS].reshape(batch, seq, N_HEADS)
    rows3 = lambda a: a.reshape(batch, seq, W_HEADS)
    cq3, kx = _fox_decay_operands(cum)
    oa = _fox_prompt(q_f, cq3, kt_f, kx, rows3(v_f))
    member, km, own_bias, prev_delta, prev_tile = _moba_side_operands(kmean, _moba_bias(t5_table), t5_table,
                                                                      batch, seq)
    ob = _moba_prompt(t5_table, q_m, q32_m, kt_m, member, rows3(v_m), km, own_bias, prev_delta, prev_tile)
    merge_w = (wbf, wbm, wout, g_ffn, w_router)
    x1, h2, route = _merge(xp, oa.reshape(n_p, W_HEADS), ob.reshape(n_p, W_HEADS), sga, sgb, *merge_w,
                           tm=MOBA_BLOCK)
    y_prompt = _moe(x1, h2, route, wg, wu, wd, tm=min(1024, n_p)).reshape(batch, seq, D_MODEL)

    xs = x_sample.reshape(n_s, D_MODEL)
    qa_s, ka_s, va_s, lf_s, qb_s, kb_s, vb_s, sga_s, sgb_s, _ = _project(xs, n_s, *proj_w)
    tokh = lambda a: a.reshape(n_s, N_HEADS, HEAD_DIM, 1)
    rows_last = lambda c: c[0].transpose(0, 2, 3, 1)
    oa_s = _fox_decode(page_table, tokh(qa_s), tokh(ka_s), tokh(va_s), lf_s[:, :N_HEADS, None],
                       rows_last(cache_fox_k), rows_last(cache_fox_v), cache_fox_logf[0].transpose(0, 2, 1))
    ob_s = _moba_decode(page_table, tokh(qb_s), tokh(kb_s), tokh(vb_s), t5_table[:, :, None],
                        rows_last(cache_moba_k), rows_last(cache_moba_v))
    flat = lambda o: o.reshape(n_s, W_HEADS).astype(BF16)
    x1_s, h2_s, route_s = _merge(xs, flat(oa_s), flat(ob_s), sga_s, sgb_s, *merge_w, tm=n_s)
    y_sample = _moe(x1_s, h2_s, route_s, wg, wu, wd, tm=n_s).reshape(nb, nt, D_MODEL)

    kv_p = lambda a: a.transpose(0, 3, 1, 2)[None]
    kv_s = lambda a: a.reshape(depth, nb, nt, N_HEADS, HEAD_DIM)
    return (y_prompt, y_sample,
            kv_p(kt_f32), kv_p(vt_f32), lf[:, :N_HEADS].reshape(depth, batch, seq, N_HEADS),
            kv_p(kt_m32), kv_p(vt_m32),
            kv_s(ka_s), kv_s(va_s), lf_s[:, :N_HEADS].reshape(depth, nb, nt, N_HEADS), kv_s(kb_s), kv_s(vb_s))
```

```python
import functools
import math

import numpy as np
import jax
import jax.numpy as jnp
from jax import lax
from jax.experimental import pallas as pl
from jax.experimental.pallas import tpu as pltpu

F32 = jnp.float32
BF16 = jnp.bfloat16
HIGHEST = lax.Precision.HIGHEST

D_MODEL = 1024
N_HEADS = 8
HEAD_DIM = 64
W_HEADS = N_HEADS * HEAD_DIM
MOBA_BLOCK = 256
MOBA_TOPK = 3
T5_BUCKETS = 32
T5_MAX_DIST = 128
N_GROUPS = 4
EXPERTS_PER_GROUP = 8
N_EXPERTS = N_GROUPS * EXPERTS_PER_GROUP
D_EXPERT = 256
EPS = 1e-6
NEG = -1e30
SCALE = HEAD_DIM ** -0.5
LOG2E = 1.4426950408889634
LANES = 128
ATTN_TILE = 2 * MOBA_BLOCK
VMEM_LIMIT = 56 * 1024 * 1024


def _t5_thresholds():
    max_exact = T5_BUCKETS // 2
    n = np.arange(0, 4 * T5_MAX_DIST)
    nf = np.maximum(n, max_exact).astype(np.float32)
    large = max_exact + (np.log(nf / np.float32(max_exact)) / np.float32(math.log(T5_MAX_DIST / max_exact))
                         * np.float32(T5_BUCKETS - max_exact)).astype(np.int32)
    bucket = np.where(n < max_exact, n, np.minimum(large, T5_BUCKETS - 1))
    return [int(np.argmax(bucket >= j)) for j in range(1, T5_BUCKETS)]


T5_THRESHOLDS = _t5_thresholds()


def _params(*sem):
    return pltpu.CompilerParams(dimension_semantics=sem, vmem_limit_bytes=VMEM_LIMIT)


def _proj_body(x_ref, g_ref, w_ref, wf_ref, bf_ref, gains_ref, seg_ref, precise_norm=False):
    x = x_ref[...]
    h = x * lax.rsqrt(jnp.mean(x * x, axis=-1, keepdims=True) + EPS) * g_ref[...]
    hb = h.astype(BF16)
    seg = seg_ref[...]

    def proj(i, width=W_HEADS):
        return jnp.dot(hb, w_ref[:, i:i + width], preferred_element_type=F32)

    def headnorm(z, row):
        if precise_norm:
            ms = jnp.dot(z * z, seg.astype(F32), preferred_element_type=F32, precision=HIGHEST)
        else:
            ms = jnp.dot((z * z).astype(BF16), seg, preferred_element_type=F32)
        return z * lax.rsqrt(ms + EPS) * gains_ref[row:row + 1, :]

    f = jnp.dot(hb, wf_ref[...], preferred_element_type=F32) + bf_ref[...]
    return dict(
        qa=headnorm(proj(0), 0), ka=headnorm(proj(W_HEADS), 1), va=proj(2 * W_HEADS),
        qb=headnorm(proj(3 * W_HEADS), 2), kb=headnorm(proj(4 * W_HEADS), 3), vb=proj(5 * W_HEADS),
        sga=jax.nn.sigmoid(proj(6 * W_HEADS, D_MODEL)),
        sgb=jax.nn.sigmoid(proj(6 * W_HEADS + D_MODEL, D_MODEL)),
        lf=jnp.minimum(f, 0.0) - jnp.log1p(jnp.exp(-jnp.abs(f))))


def _proj_kernel(x_ref, g_ref, w_ref, wf_ref, bf_ref, gains_ref, seg_ref,
                 qa_ref, ka_ref, va_ref, lf_ref, qb_ref, kb_ref, vb_ref, sga_ref, sgb_ref, km_ref):
    z = _proj_body(x_ref, g_ref, w_ref, wf_ref, bf_ref, gains_ref, seg_ref, precise_norm=True)
    for name, ref in (("qa", qa_ref), ("ka", ka_ref), ("va", va_ref), ("lf", lf_ref), ("qb", qb_ref),
                      ("kb", kb_ref), ("vb", vb_ref), ("sga", sga_ref), ("sgb", sgb_ref)):
        ref[...] = z[name]
    km_ref[...] = jnp.mean(z["kb"], axis=0, keepdims=True)


def _proj_prompt_kernel(x_ref, g_ref, w_ref, wf_ref, bf_ref, gains_ref, seg_ref,
                        qf_ref, ktf_ref, vf_ref, ktf32_ref, vtf32_ref, lf_ref,
                        qm_ref, qm32_ref, ktm_ref, vm_ref, ktm32_ref, vtm32_ref, km_ref, sga_ref, sgb_ref):
    z = _proj_body(x_ref, g_ref, w_ref, wf_ref, bf_ref, gains_ref, seg_ref)
    tm = x_ref.shape[0]
    lane = lax.broadcasted_iota(jnp.int32, (tm, LANES), 1)
    low = lane < HEAD_DIM
    ones = jnp.where(jnp.logical_and(lane >= HEAD_DIM, lane < HEAD_DIM + 3), 1.0, 0.0)

    def heads_low(a):
        out = []
        for p in range(N_HEADS // 2):
            pair = a[:, p * LANES:(p + 1) * LANES]
            out += [pair, pltpu.roll(pair, HEAD_DIM, 1)]
        return out

    def heads_t(a):
        return a.T.reshape(N_HEADS, HEAD_DIM, tm)

    for hh, q in enumerate(heads_low(z["qa"] * (SCALE * LOG2E))):
        qf_ref[hh] = jnp.where(low, q, ones).astype(BF16)
    for hh, q in enumerate(heads_low(z["qb"])):
        qm32_ref[hh] = q[:, :HEAD_DIM]
        qm_ref[hh] = jnp.where(low, q * (SCALE * LOG2E), 0.0).astype(BF16)
    kat, kbt = heads_t(z["ka"]), heads_t(z["kb"])
    ktf32_ref[...] = kat
    ktf_ref[...] = kat.astype(BF16)
    ktm32_ref[...] = kbt
    ktm_ref[...] = kbt.astype(BF16)
    vtf32_ref[...] = heads_t(z["va"])
    vtm32_ref[...] = heads_t(z["vb"])
    vf_ref[...] = z["va"].astype(BF16)
    vm_ref[...] = z["vb"].astype(BF16)
    km_ref[...] = jnp.mean(z["kb"], axis=0, keepdims=True)
    lf_ref[...] = z["lf"]
    sga_ref[...] = z["sga"].astype(BF16)
    sgb_ref[...] = z["sgb"].astype(BF16)


def _project_prompt(x2d, batch, seq, tm, g_attn, w_main, w_f, b_f, gains, seg):
    n = x2d.shape[0]
    tpb = seq // tm
    row = lambda i: (i, 0)
    const = lambda i: (0, 0)
    blk = lambda w: pl.BlockSpec((tm, w), row)
    qspec = lambda w: pl.BlockSpec((None, N_HEADS, tm, w), lambda i: (i // tpb, 0, i % tpb, 0))
    tspec = pl.BlockSpec((None, N_HEADS, HEAD_DIM, tm), lambda i: (i // tpb, 0, 0, i % tpb))
    hm = lambda w, dt: jax.ShapeDtypeStruct((batch, N_HEADS, seq, w), dt)
    tr = lambda dt: jax.ShapeDtypeStruct((batch, N_HEADS, HEAD_DIM, seq), dt)
    wide = lambda w, dt: jax.ShapeDtypeStruct((n, w), dt)
    out_shape = (hm(LANES, BF16), tr(BF16), wide(W_HEADS, BF16), tr(F32), tr(F32), wide(LANES, F32),
                 hm(LANES, BF16), hm(HEAD_DIM, F32), tr(BF16), wide(W_HEADS, BF16), tr(F32), tr(F32),
                 jax.ShapeDtypeStruct((n // tm, 1, W_HEADS), F32), wide(D_MODEL, BF16), wide(D_MODEL, BF16))
    out_specs = (qspec(LANES), tspec, blk(W_HEADS), tspec, tspec, blk(LANES),
                 qspec(LANES), qspec(HEAD_DIM), tspec, blk(W_HEADS), tspec, tspec,
                 pl.BlockSpec((None, 1, W_HEADS), lambda i: (i, 0, 0)), blk(D_MODEL), blk(D_MODEL))
    in_specs = [blk(D_MODEL), pl.BlockSpec((1, D_MODEL), const), pl.BlockSpec(w_main.shape, const),
                pl.BlockSpec(w_f.shape, const), pl.BlockSpec((1, LANES), const),
                pl.BlockSpec(gains.shape, const), pl.BlockSpec(seg.shape, const)]
    return pl.pallas_call(_proj_prompt_kernel, grid=(n // tm,), in_specs=in_specs, out_specs=out_specs,
                          out_shape=out_shape, compiler_params=_params("arbitrary"), name="proj_prompt")(
        x2d, g_attn, w_main, w_f, b_f, gains, seg)


def _project(x2d, tm, g_attn, w_main, w_f, b_f, gains, seg):
    n = x2d.shape[0]
    row = lambda i: (i, 0)
    const = lambda i: (0, 0)
    wide = lambda w, dt: jax.ShapeDtypeStruct((n, w), dt)
    out_shape = (wide(W_HEADS, F32), wide(W_HEADS, F32), wide(W_HEADS, F32), wide(LANES, F32),
                 wide(W_HEADS, F32), wide(W_HEADS, F32), wide(W_HEADS, F32),
                 wide(D_MODEL, F32), wide(D_MODEL, F32),
                 jax.ShapeDtypeStruct((n // tm, 1, W_HEADS), F32))
    blk = lambda w: pl.BlockSpec((tm, w), row)
    out_specs = (blk(W_HEADS), blk(W_HEADS), blk(W_HEADS), blk(LANES), blk(W_HEADS), blk(W_HEADS), blk(W_HEADS),
                 blk(D_MODEL), blk(D_MODEL), pl.BlockSpec((None, 1, W_HEADS), lambda i: (i, 0, 0)))
    in_specs = [blk(D_MODEL), pl.BlockSpec((1, D_MODEL), const), pl.BlockSpec(w_main.shape, const),
                pl.BlockSpec(w_f.shape, const), pl.BlockSpec((1, LANES), const),
                pl.BlockSpec(gains.shape, const), pl.BlockSpec(seg.shape, const)]
    return pl.pallas_call(_proj_kernel, grid=(n // tm,), in_specs=in_specs, out_specs=out_specs,
                          out_shape=out_shape, compiler_params=_params("arbitrary"), name="proj")(
        x2d, g_attn, w_main, w_f, b_f, gains, seg)


def _cumsum_kernel(lf_ref, o_ref, carry_ref, *, tc):
    @pl.when(pl.program_id(1) == 0)
    def _():
        carry_ref[...] = jnp.zeros_like(carry_ref)

    r = lax.broadcasted_iota(jnp.int32, (tc, tc), 0)
    c = lax.broadcasted_iota(jnp.int32, (tc, tc), 1)
    tri = jnp.where(c <= r, 1.0, 0.0).astype(F32)
    out = jnp.dot(tri, lf_ref[...], preferred_element_type=F32, precision=HIGHEST) + carry_ref[...]
    o_ref[...] = out
    carry_ref[...] = out[tc - 1:tc, :]


def _cumsum(lf, batch, seq, tc=256):
    nchunk = seq // tc
    spec = pl.BlockSpec((tc, LANES), lambda b, i: (b * nchunk + i, 0))
    return pl.pallas_call(functools.partial(_cumsum_kernel, tc=tc), grid=(batch, nchunk),
                          in_specs=[spec], out_specs=spec,
                          out_shape=jax.ShapeDtypeStruct(lf.shape, F32),
                          scratch_shapes=[pltpu.VMEM((1, LANES), F32)],
                          compiler_params=_params("arbitrary", "arbitrary"), name="logf_cumsum")(lf)


def _online_update(carry, s2, v):
    m, l, acc = carry
    m_new = jnp.maximum(m, jnp.max(s2, axis=-1, keepdims=True))
    alpha = jnp.exp2(m - m_new)
    p = jnp.exp2(s2 - m_new)
    l = alpha * l + jnp.sum(p, axis=-1, keepdims=True)
    acc = alpha * acc + jnp.dot(p.astype(BF16), v, preferred_element_type=F32)
    return m_new, l, acc


def _softmax_init(rows, width=LANES):
    return (jnp.full((rows, 1), NEG, F32), jnp.zeros((rows, 1), F32), jnp.zeros((rows, width), F32))


def _pair_output(results):
    (_, l0, a0), (_, l1, a1) = results
    lane = lax.broadcasted_iota(jnp.int32, a0.shape, 1)
    return jnp.where(lane < HEAD_DIM, a0 / l0, a1 / l1)


def _split3(c):
    def top(x):
        bits = lax.bitcast_convert_type(x, jnp.uint32) & jnp.uint32(0xFFFF0000)
        return lax.bitcast_convert_type(bits, F32)
    hi = top(c)
    mid = top(c - hi)
    return hi, mid, c - hi - mid


FOX_EXTRA_ROWS = 16


def _fox_kernel(q_ref, cum_ref, kt_ref, kx_ref, v_ref, o_ref, *, tq):
    pair, qi = pl.program_id(1), pl.program_id(2)
    lane = lax.broadcasted_iota(jnp.int32, (tq, LANES), 1)
    row = lax.broadcasted_iota(jnp.int32, (tq, tq), 0)
    col = lax.broadcasted_iota(jnp.int32, (tq, tq), 1)
    cum = cum_ref[...] * LOG2E
    results = []
    for hh in range(2):
        cq = jnp.sum(jnp.where(lane == 2 * pair + hh, cum, 0.0), axis=-1, keepdims=True)
        q = q_ref[hh].astype(F32)
        for piece, value in enumerate(_split3(cq)):
            q = jnp.where(lane == HEAD_DIM + 3 + piece, value, q)
        q = q.astype(BF16)

        def tile(k0, width=tq, q=q, hh=hh):
            k0 = pl.multiple_of(k0, tq)
            kt = jnp.concatenate([kt_ref[hh, :, pl.ds(k0, width)], kx_ref[hh, :, pl.ds(k0, width)],
                                  jnp.zeros((LANES - HEAD_DIM - FOX_EXTRA_ROWS, width), BF16)], axis=0)
            return jnp.dot(q, kt, preferred_element_type=F32), v_ref[pl.ds(k0, width), :]

        carry = lax.fori_loop(0, qi // 2, lambda c, carry, tile=tile: _online_update(carry, *tile(c * 2 * tq, 2 * tq)),
                              _softmax_init(tq))
        carry = lax.fori_loop(qi - qi % 2, qi, lambda c, carry, tile=tile: _online_update(carry, *tile(c * tq)), carry)
        s2, v = tile(qi * tq)
        results.append(_online_update(carry, jnp.where(col <= row, s2, NEG), v))
    o_ref[...] = _pair_output(results).astype(o_ref.dtype)


def _fox_prompt(q_aug, cum, kt, kx, v, tq=ATTN_TILE):
    b, h, s, _ = q_aug.shape
    pair = lambda shape, imap: pl.BlockSpec((None, 2) + shape, imap)
    in_specs = [pair((tq, LANES), lambda b, p, i: (b, p, i, 0)),
                pl.BlockSpec((None, tq, LANES), lambda b, p, i: (b, i, 0)),
                pair((HEAD_DIM, s), lambda b, p, i: (b, p, 0, 0)),
                pair((FOX_EXTRA_ROWS, s), lambda b, p, i: (b, p, 0, 0)),
                pl.BlockSpec((None, s, LANES), lambda b, p, i: (b, 0, p))]
    out_specs = pl.BlockSpec((None, tq, LANES), lambda b, p, i: (b, i, p))
    return pl.pallas_call(functools.partial(_fox_kernel, tq=tq), grid=(b, h // 2, s // tq), in_specs=in_specs,
                          out_specs=out_specs, out_shape=jax.ShapeDtypeStruct((b, s, h * HEAD_DIM), BF16),
                          compiler_params=_params("arbitrary", "arbitrary", "arbitrary"), name="fox_prompt")(
        q_aug, cum, kt, kx, v)


def _fox_key_rows(cum):
    pieces = _split3((cum * LOG2E).transpose(0, 2, 1))
    ones = jnp.ones_like(pieces[0])
    zero = jnp.zeros_like(pieces[0])
    return jnp.stack([-p for p in pieces] + [ones] * 3 + [zero] * (FOX_EXTRA_ROWS - 6), axis=2).astype(BF16)


def _t5_select(dist, table_at):
    bias = table_at(0)
    for j, thr in enumerate(T5_THRESHOLDS, start=1):
        bias = jnp.where(dist >= thr, table_at(j), bias)
    return bias


def _bias_kernel(tab_ref, o_ref):
    hh = pl.program_id(0)
    i = lax.broadcasted_iota(jnp.int32, (MOBA_BLOCK, 2 * MOBA_BLOCK), 0)
    j = lax.broadcasted_iota(jnp.int32, (MOBA_BLOCK, 2 * MOBA_BLOCK), 1)
    dist = MOBA_BLOCK + i - j
    o_ref[...] = _t5_select(dist, lambda bkt: tab_ref[bkt, hh])


def _moba_bias(t5_table):
    return pl.pallas_call(_bias_kernel, grid=(N_HEADS,),
                          in_specs=[pl.BlockSpec(memory_space=pltpu.SMEM)],
                          out_specs=pl.BlockSpec((None, MOBA_BLOCK, 2 * MOBA_BLOCK), lambda h: (h, 0, 0)),
                          out_shape=jax.ShapeDtypeStruct((N_HEADS, MOBA_BLOCK, 2 * MOBA_BLOCK), F32),
                          compiler_params=_params("arbitrary"), name="moba_bias")(t5_table)


MOBA_MAX_BLOCKS = 32
SEL_LANE0 = HEAD_DIM
SEL_LANE1 = HEAD_DIM + MOBA_MAX_BLOCKS


def _moba_kernel(tab_ref, qp_ref, q32_ref, kt_ref, member_ref, v_ref, km_ref, own_ref, prev_ref, prevtile_ref,
                 o_ref, qsel_ref):
    blk, tq = MOBA_BLOCK, 2 * MOBA_BLOCK
    pair, i = pl.program_id(1), pl.program_id(2)
    nblk = MOBA_MAX_BLOCKS
    dot = functools.partial(jnp.dot, preferred_element_type=F32)
    blk_i = lax.broadcasted_iota(jnp.int32, (nblk, tq), 0)
    blk_f = blk_i.astype(F32)
    own = 2 * i + (lax.broadcasted_iota(jnp.int32, (nblk, tq), 1) >= blk).astype(jnp.int32)
    lane = lax.broadcasted_iota(jnp.int32, (tq, LANES), 1)
    causal = lax.broadcasted_iota(jnp.int32, (blk, blk), 1) <= lax.broadcasted_iota(jnp.int32, (blk, blk), 0)
    k0 = pl.multiple_of(i * tq, tq)
    v_d = v_ref[pl.ds(k0, tq), :]
    results = []
    for hh in range(2):
        qp = qp_ref[hh]
        gate = lax.dot_general(km_ref[hh].astype(BF16), q32_ref[hh].astype(BF16), (((1,), (1,)), ((), ())),
                               preferred_element_type=F32)
        g = jnp.where(blk_i < own, gate, -jnp.inf)
        chosen = jnp.zeros((nblk, tq), F32)
        for _ in range(MOBA_TOPK):
            mx = jnp.max(g, axis=0, keepdims=True)
            idx = jnp.min(jnp.where(g == mx, blk_f, float(nblk)), axis=0, keepdims=True)
            pick = jnp.logical_and(blk_f == idx, mx > -jnp.inf)
            chosen = jnp.where(pick, 1.0, chosen)
            g = jnp.where(pick, -jnp.inf, g)
        far = jnp.full((1, 1), tab_ref[T5_BUCKETS - 1, 2 * pair + hh] * LOG2E, F32)
        far_hi = far.astype(BF16).astype(F32)
        picked = chosen > 0.5
        sel_t = jnp.concatenate([jnp.zeros((SEL_LANE0, tq), F32), jnp.where(picked, far_hi, NEG),
                                 jnp.where(picked, far - far_hi, 0.0)], axis=0)
        qsel_ref[hh] = jnp.where(lane >= SEL_LANE0, sel_t.T.astype(BF16), qp)

        def keys(k0, width=tq, hh=hh):
            return jnp.concatenate([kt_ref[hh, :, pl.ds(k0, width)], member_ref[:, pl.ds(k0, width)]], axis=0)

        kt_d = keys(k0)
        own_bias = own_ref[hh]
        s_a = jnp.where(causal, dot(qp[:blk], kt_d[:, :blk]) + own_bias, NEG)
        s_b = dot(qsel_ref[hh, blk:, :], kt_d[:, :blk]) + prev_ref[hh]
        s_c = jnp.where(causal, dot(qp[blk:], kt_d[:, blk:]) + own_bias, NEG)
        s_d = jnp.concatenate([jnp.concatenate([s_a, jnp.full((blk, blk), NEG, F32)], axis=1),
                               jnp.concatenate([s_b, s_c], axis=1)], axis=0)
        carry = _online_update(_softmax_init(tq), s_d, v_d)

        def tile(c, carry, near=False, tiles=1, hh=hh, keys=keys):
            k0 = pl.multiple_of(c * tiles * tq, tq)
            s2 = dot(qsel_ref[hh], keys(k0, tiles * tq))
            if near:
                s2 = s2 + prevtile_ref[hh]
            return _online_update(carry, s2, v_ref[pl.ds(k0, tiles * tq), :])

        n_far = jnp.maximum(i - 1, 0)
        carry = lax.fori_loop(n_far, i, functools.partial(tile, near=True), carry)
        carry = lax.fori_loop(0, n_far // 2, functools.partial(tile, tiles=2), carry)
        results.append(lax.fori_loop(n_far - n_far % 2, n_far, tile, carry))
    o_ref[...] = _pair_output(results).astype(o_ref.dtype)


def _moba_prompt(t5_table, q_pad, q32, kt, member, v, km, own_bias, prev_delta, prev_tile):
    b, h, s, _ = q_pad.shape
    blk, tq = MOBA_BLOCK, 2 * MOBA_BLOCK
    pair = lambda shape, imap: pl.BlockSpec((None, 2) + shape, imap)
    per_pair = lambda shape: pl.BlockSpec((2,) + shape, lambda b, p, i: (p, 0, 0))
    in_specs = [pl.BlockSpec(memory_space=pltpu.SMEM),
                pair((tq, LANES), lambda b, p, i: (b, p, i, 0)),
                pair((tq, HEAD_DIM), lambda b, p, i: (b, p, i, 0)),
                pair((HEAD_DIM, s), lambda b, p, i: (b, p, 0, 0)),
                pl.BlockSpec((2 * MOBA_MAX_BLOCKS, s), lambda b, p, i: (0, 0)),
                pl.BlockSpec((None, s, LANES), lambda b, p, i: (b, 0, p)),
                pair((MOBA_MAX_BLOCKS, HEAD_DIM), lambda b, p, i: (b, p, 0, 0)),
                per_pair((blk, blk)), per_pair((blk, blk)), per_pair((tq, tq))]
    out_specs = pl.BlockSpec((None, tq, LANES), lambda b, p, i: (b, i, p))
    return pl.pallas_call(_moba_kernel, grid=(b, h // 2, s // tq), in_specs=in_specs, out_specs=out_specs,
                          out_shape=jax.ShapeDtypeStruct((b, s, h * HEAD_DIM), BF16),
                          scratch_shapes=[pltpu.VMEM((2, tq, LANES), BF16)],
                          compiler_params=_params("arbitrary", "arbitrary", "arbitrary"), name="moba_prompt")(
        t5_table, q_pad, q32, kt, member, v, km, own_bias, prev_delta, prev_tile)


def _moba_side_operands(kmean, bias_tiles, t5_table, batch, seq):
    blk = MOBA_BLOCK
    nb = seq // blk
    assert nb <= MOBA_MAX_BLOCKS
    member = (jnp.arange(MOBA_MAX_BLOCKS)[:, None] == jnp.arange(seq)[None] // blk).astype(BF16)
    member = jnp.concatenate([member, member], axis=0)
    km = kmean.reshape(batch, nb, N_HEADS, HEAD_DIM).transpose(0, 2, 1, 3)
    km = jnp.pad(km, ((0, 0), (0, 0), (0, MOBA_MAX_BLOCKS - nb), (0, 0)))
    far = t5_table[T5_BUCKETS - 1][:, None, None]
    own_bias = bias_tiles[:, :, blk:] * LOG2E
    prev_delta = (bias_tiles[:, :, :blk] - far) * LOG2E
    prev_tile = jnp.zeros((N_HEADS, 2 * blk, 2 * blk), F32).at[:, :blk, blk:].set(prev_delta)
    return member, km, own_bias, prev_delta, prev_tile


def _merge_kernel(x_ref, oa_ref, ob_ref, sga_ref, sgb_ref, wbf_ref, wbm_ref, wout_ref, g_ref, wr_ref,
                  x1_ref, h2_ref, route_ref):
    ma = jnp.dot(oa_ref[...], wbf_ref[...], preferred_element_type=F32)
    mb = jnp.dot(ob_ref[...], wbm_ref[...], preferred_element_type=F32)
    mix = sga_ref[...].astype(F32) * ma + sgb_ref[...].astype(F32) * mb
    x1 = x_ref[...] + jnp.dot(mix.astype(BF16), wout_ref[...], preferred_element_type=F32)
    x1_ref[...] = x1
    h2 = x1 * lax.rsqrt(jnp.mean(x1 * x1, axis=-1, keepdims=True) + EPS) * g_ref[...]
    h2_ref[...] = h2.astype(BF16)

    logits = jnp.dot(h2.astype(BF16), wr_ref[...].astype(BF16), preferred_element_type=F32)
    lane = lax.broadcasted_iota(jnp.int32, logits.shape, 1)
    big = LANES

    def argmax_low(vals):
        mx = jnp.max(vals, axis=-1, keepdims=True)
        return mx, jnp.min(jnp.where(vals == mx, lane, big), axis=-1, keepdims=True)

    glog = jnp.where(lane < N_GROUPS, logits, -jnp.inf)
    gmax, gsel = argmax_low(glog)
    gprob = 1.0 / jnp.sum(jnp.exp(glog - gmax), axis=-1, keepdims=True)
    first = N_GROUPS + gsel * EXPERTS_PER_GROUP
    in_group = jnp.logical_and(lane >= first, lane < first + EXPERTS_PER_GROUP)
    elog = jnp.where(in_group, logits, -jnp.inf)
    v0, i0 = argmax_low(elog)
    v1, i1 = argmax_low(jnp.where(lane == i0, -jnp.inf, elog))
    e1 = jnp.exp(v1 - v0)
    w0 = gprob / (1.0 + e1)
    w1 = gprob * e1 / (1.0 + e1)
    route = jnp.where(lane == 0, (i0 - N_GROUPS).astype(F32),
                      jnp.where(lane == 1, (i1 - N_GROUPS).astype(F32),
                                jnp.where(lane == 2, w0, jnp.where(lane == 3, w1, 0.0))))
    route_ref[...] = route


def _merge(x2d, oa, ob, sga, sgb, wbf, wbm, wout, g_ffn, w_router, tm):
    n = x2d.shape[0]
    row = lambda i: (i, 0)
    const = lambda i: (0, 0)
    blk = lambda w: pl.BlockSpec((tm, w), row)
    full = lambda a: pl.BlockSpec(a.shape, const)
    in_specs = [blk(D_MODEL), blk(W_HEADS), blk(W_HEADS), blk(D_MODEL), blk(D_MODEL),
                full(wbf), full(wbm), full(wout), full(g_ffn), full(w_router)]
    out_shape = (jax.ShapeDtypeStruct((n, D_MODEL), F32), jax.ShapeDtypeStruct((n, D_MODEL), BF16),
                 jax.ShapeDtypeStruct((n, LANES), F32))
    return pl.pallas_call(_merge_kernel, grid=(n // tm,), in_specs=in_specs,
                          out_specs=(blk(D_MODEL), blk(D_MODEL), blk(LANES)), out_shape=out_shape,
                          compiler_params=_params("arbitrary"), name="merge_router")(
        x2d, oa, ob, sga, sgb, wbf, wbm, wout, g_ffn, w_router)


def _moe_kernel(x1_ref, h2_ref, route_ref, wg_ref, wu_ref, wd_ref, o_ref):
    e = pl.program_id(1)

    @pl.when(e == 0)
    def _():
        o_ref[...] = x1_ref[...]

    h = h2_ref[...]
    a = jax.nn.silu(jnp.dot(h, wg_ref[...], preferred_element_type=F32)) * \
        jnp.dot(h, wu_ref[...], preferred_element_type=F32)
    y = jnp.dot(a.astype(BF16), wd_ref[...], preferred_element_type=F32)
    r = route_ref[...]
    ef = e.astype(F32)
    gate = jnp.where(r[:, 0:1] == ef, r[:, 2:3], 0.0) + jnp.where(r[:, 1:2] == ef, r[:, 3:4], 0.0)
    o_ref[...] += y * gate


def _moe(x1, h2, route, wg, wu, wd, tm):
    n = x1.shape[0]
    row = lambda i, e: (i, 0)
    in_specs = [pl.BlockSpec((tm, D_MODEL), row), pl.BlockSpec((tm, D_MODEL), row), pl.BlockSpec((tm, LANES), row),
                pl.BlockSpec((None, D_MODEL, D_EXPERT), lambda i, e: (e, 0, 0)),
                pl.BlockSpec((None, D_MODEL, D_EXPERT), lambda i, e: (e, 0, 0)),
                pl.BlockSpec((None, D_EXPERT, D_MODEL), lambda i, e: (e, 0, 0))]
    return pl.pallas_call(_moe_kernel, grid=(n // tm, N_EXPERTS), in_specs=in_specs,
                          out_specs=pl.BlockSpec((tm, D_MODEL), row),
                          out_shape=jax.ShapeDtypeStruct((n, D_MODEL), F32),
                          compiler_params=_params("arbitrary", "arbitrary"), name="experts")(
        x1, h2, route, wg, wu, wd)


DECODE_PAGES_PER_STEP = 8


def _as_mxu(x):
    return x.astype(BF16).astype(F32)


def _page_logits(kt, q3):
    return jnp.sum(_as_mxu(kt) * q3, axis=1)


def _fox_decode_kernel(pt_ref, q_ref, kn_ref, vn_ref, lfn_ref, *rest, page, group):
    k_refs, v_refs, lf_refs = rest[:group], rest[group:2 * group], rest[2 * group:3 * group]
    o_ref, s_ref, m_ref, l_ref, acc_ref, carry_ref = rest[3 * group:]
    phase, j = pl.program_id(1), pl.program_id(2)
    q3 = _as_mxu(q_ref[...]) * SCALE
    s_new = jnp.sum(q3 * _as_mxu(kn_ref[...]), axis=1)

    @pl.when(phase == 0)
    def _():
        @pl.when(j == 0)
        def _():
            m_ref[...] = s_new
            carry_ref[...] = lfn_ref[...]

        r0 = lax.broadcasted_iota(jnp.int32, (page, page), 0)
        r1 = lax.broadcasted_iota(jnp.int32, (page, page), 1)
        later = jnp.where(r0 > r1, 1.0, 0.0).astype(F32)
        carry = carry_ref[...]
        m = m_ref[...]
        for g in range(group):
            lf = lf_refs[g][...]
            decay = jnp.dot(lf, later, preferred_element_type=F32, precision=HIGHEST) + carry
            s = _page_logits(k_refs[g][...], q3) + decay
            s_ref[pl.ds(j * group + g, 1)] = s[None]
            m = jnp.maximum(m, jnp.max(s, axis=-1, keepdims=True))
            carry = carry + jnp.sum(lf, axis=-1, keepdims=True)
        carry_ref[...] = carry
        m_ref[...] = m

    @pl.when(phase == 1)
    def _():
        m = m_ref[...]

        @pl.when(j == 0)
        def _():
            e = jnp.exp(s_ref[...] - m[None])
            l = jnp.sum(jnp.sum(e, axis=0), axis=-1, keepdims=True) + jnp.exp(s_new - m)
            l_ref[...] = l
            lane = lax.broadcasted_iota(jnp.int32, acc_ref.shape, 2)
            p_new = _as_mxu(jnp.exp(s_new - m) / l)
            acc_ref[...] = jnp.where(lane == 0, p_new[:, :, None] * _as_mxu(vn_ref[...]), 0.0)

        l = l_ref[...]
        acc = acc_ref[...]
        for g in range(group):
            p = _as_mxu(jnp.exp(s_ref[j * group + g] - m) / l)
            acc = acc + _as_mxu(v_refs[g][...]) * p[:, None, :]
        acc_ref[...] = acc

        @pl.when(j == pl.num_programs(2) - 1)
        def _():
            o_ref[...] = jnp.sum(acc_ref[...], axis=-1, keepdims=True)


def _fox_decode(page_table, q, kn, vn, lfn, kt_cache, vt_cache, lft_cache):
    nb, n_pages = page_table.shape
    page = kt_cache.shape[-1]
    group = math.gcd(DECODE_PAGES_PER_STEP, n_pages)
    steps = n_pages // group
    tok4 = lambda b, ph, j, pt: (b, 0, 0, 0)
    tok3 = lambda b, ph, j, pt: (b, 0, 0)

    def paged(g, ndim, key_side):
        def index(b, ph, j, pt):
            step = j * (1 - ph) + (steps - 1) * ph if key_side else j * ph
            return (pt[b, n_pages - 1 - (step * group + g)],) + (0,) * ndim
        return index

    col = pl.BlockSpec((None, N_HEADS, HEAD_DIM, 1), tok4)
    k_specs = [pl.BlockSpec((None, N_HEADS, HEAD_DIM, page), paged(g, 3, True)) for g in range(group)]
    v_specs = [pl.BlockSpec((None, N_HEADS, HEAD_DIM, page), paged(g, 3, False)) for g in range(group)]
    lf_specs = [pl.BlockSpec((None, N_HEADS, page), paged(g, 2, True)) for g in range(group)]
    in_specs = [col, col, col, pl.BlockSpec((None, N_HEADS, 1), tok3)] + k_specs + v_specs + lf_specs
    grid_spec = pltpu.PrefetchScalarGridSpec(
        num_scalar_prefetch=1, grid=(nb, 2, steps), in_specs=in_specs, out_specs=col,
        scratch_shapes=[pltpu.VMEM((n_pages, N_HEADS, page), F32),
                        pltpu.VMEM((N_HEADS, 1), F32), pltpu.VMEM((N_HEADS, 1), F32),
                        pltpu.VMEM((N_HEADS, HEAD_DIM, page), F32), pltpu.VMEM((N_HEADS, 1), F32)])
    return pl.pallas_call(functools.partial(_fox_decode_kernel, page=page, group=group), grid_spec=grid_spec,
                          out_shape=jax.ShapeDtypeStruct((nb, N_HEADS, HEAD_DIM, 1), F32),
                          compiler_params=_params("arbitrary", "arbitrary", "arbitrary"), name="fox_decode")(
        page_table, q, kn, vn, lfn, *([kt_cache] * group), *([vt_cache] * group), *([lft_cache] * group))


def _moba_score_kernel(pt_ref, q_ref, kn_ref, vn_ref, tab_ref, *rest, page, n_pages, group):
    k_refs = rest[:group]
    p_ref, sel_ref, own_ref, s_ref, gate_ref = rest[group:]
    j = pl.program_id(1)
    ppb = MOBA_BLOCK // page
    nblk = n_pages // ppb
    p_len = n_pages * page
    q3 = _as_mxu(q_ref[...])
    for g in range(group):
        s_ref[pl.ds(j * group + g, 1)] = _page_logits(k_refs[g][...], q3)[None]
    for g0 in range(0, group, ppb):
        kblk = k_refs[g0][...]
        for t in range(1, ppb):
            kblk = kblk + k_refs[g0 + t][...]
        kmean = jnp.sum(kblk, axis=-1, keepdims=True) * (1.0 / MOBA_BLOCK)
        gate_ref[pl.ds((j * group + g0) // ppb, 1)] = jnp.sum(q3 * _as_mxu(kmean), axis=1)[None]

    @pl.when(j == pl.num_programs(1) - 1)
    def _():
        s = s_ref[...].reshape(nblk, ppb, N_HEADS, page)
        gate = gate_ref[...]
        idx = lax.broadcasted_iota(jnp.int32, gate.shape, 0)
        lane = lax.broadcasted_iota(jnp.int32, (N_HEADS, LANES), 1)
        chosen = jnp.zeros(gate.shape, F32)
        picks = jnp.zeros((N_HEADS, LANES), jnp.int32)
        for r in range(min(MOBA_TOPK, nblk)):
            mx = jnp.max(gate, axis=0, keepdims=True)
            first = jnp.min(jnp.where(gate == mx, idx, nblk), axis=0, keepdims=True)
            pick = idx == first
            chosen = jnp.where(pick, 1.0, chosen)
            gate = jnp.where(pick, -jnp.inf, gate)
            picks = jnp.where(lane == r, first[0], picks)
        sel_ref[...] = picks

        logits = s * SCALE + tab_ref[T5_BUCKETS - 1][None, None]
        masked = jnp.where(chosen[:, None] > 0.5, logits, NEG).reshape(n_pages, N_HEADS, page)
        s_ref[...] = masked
        for pg in range(n_pages):
            if p_len - (pg * page + page - 1) < T5_THRESHOLDS[-1]:
                dist = p_len - pg * page - lax.broadcasted_iota(jnp.int32, (1, page), 1)
                bias = _t5_select(dist, lambda bkt: tab_ref[bkt])
                near = s[pg // ppb, pg % ppb] * SCALE + bias
                s_ref[pg] = jnp.where(chosen[pg // ppb] > 0.5, near, NEG)
        masked = s_ref[...]
        s_own = jnp.sum(q3 * _as_mxu(kn_ref[...]), axis=1) * SCALE + tab_ref[0]
        m = jnp.maximum(jnp.max(jnp.max(masked, axis=0), axis=-1, keepdims=True), s_own)
        p = jnp.exp(masked - m[None])
        p_own = jnp.exp(s_own - m)
        l = jnp.sum(jnp.sum(p, axis=0), axis=-1, keepdims=True) + p_own
        p_ref[...] = _as_mxu(p / l[None])
        own_ref[...] = _as_mxu(p_own / l)[:, :, None] * _as_mxu(vn_ref[...])


def _moba_gather_kernel(vpage_ref, lpage_ref, *rest):
    v_refs, p_refs = rest[:N_HEADS], rest[N_HEADS:2 * N_HEADS]
    own_ref, o_ref, acc_ref = rest[2 * N_HEADS:]
    t = pl.program_id(1)

    @pl.when(t == 0)
    def _():
        acc_ref[...] = jnp.zeros_like(acc_ref)

    for h in range(N_HEADS):
        acc_ref[h] += _as_mxu(v_refs[h][...]) * p_refs[h][h:h + 1, :]

    @pl.when(t == pl.num_programs(1) - 1)
    def _():
        o_ref[...] = jnp.sum(acc_ref[...], axis=-1, keepdims=True) + own_ref[...]


def _moba_decode(page_table, q, kn, vn, tab3, kt_cache, vt_cache):
    nb, n_pages = page_table.shape
    page = kt_cache.shape[-1]
    ppb = MOBA_BLOCK // page
    nblk = n_pages // ppb
    ntop = min(MOBA_TOPK, nblk)
    group = math.gcd(DECODE_PAGES_PER_STEP, n_pages)
    assert group % ppb == 0, "a grid step must hold whole MoBA blocks"
    tok4 = lambda b, j, pt: (b, 0, 0, 0)
    col = pl.BlockSpec((None, N_HEADS, HEAD_DIM, 1), tok4)
    k_specs = [pl.BlockSpec((None, N_HEADS, HEAD_DIM, page),
                            (lambda g: lambda b, j, pt: (pt[b, j * group + g], 0, 0, 0))(g)) for g in range(group)]
    in_specs = [col, col, col, pl.BlockSpec((T5_BUCKETS, N_HEADS, 1), lambda b, j, pt: (0, 0, 0))] + k_specs
    out_specs = (pl.BlockSpec((None, n_pages, N_HEADS, page), tok4),
                 pl.BlockSpec((None, N_HEADS, LANES), lambda b, j, pt: (b, 0, 0)), col)
    out_shape = (jax.ShapeDtypeStruct((nb, n_pages, N_HEADS, page), F32),
                 jax.ShapeDtypeStruct((nb, N_HEADS, LANES), jnp.int32),
                 jax.ShapeDtypeStruct((nb, N_HEADS, HEAD_DIM, 1), F32))
    grid_spec = pltpu.PrefetchScalarGridSpec(
        num_scalar_prefetch=1, grid=(nb, n_pages // group), in_specs=in_specs, out_specs=out_specs,
        scratch_shapes=[pltpu.VMEM((n_pages, N_HEADS, page), F32), pltpu.VMEM((nblk, N_HEADS, 1), F32)])
    probs, picks, own = pl.pallas_call(
        functools.partial(_moba_score_kernel, page=page, n_pages=n_pages, group=group), grid_spec=grid_spec,
        out_shape=out_shape, compiler_params=_params("arbitrary", "arbitrary"), name="moba_score")(
        page_table, q, kn, vn, tab3, *([kt_cache] * group))

    lpage = (picks[:, :, :ntop, None] * ppb + jnp.arange(ppb)[None, None, None]).reshape(nb, N_HEADS, ntop * ppb)
    vpage = jnp.take_along_axis(page_table[:, None, :], lpage, axis=2)
    nstep = ntop * ppb
    lpage, vpage = lpage.reshape(nb, N_HEADS * nstep), vpage.reshape(nb, N_HEADS * nstep)
    v_specs = [pl.BlockSpec((None, None, HEAD_DIM, page),
                            (lambda h: lambda b, t, vp, lp: (vp[b, h * nstep + t], h, 0, 0))(h))
               for h in range(N_HEADS)]
    p_specs = [pl.BlockSpec((None, None, N_HEADS, page),
                            (lambda h: lambda b, t, vp, lp: (b, lp[b, h * nstep + t], 0, 0))(h))
               for h in range(N_HEADS)]
    col2 = pl.BlockSpec((None, N_HEADS, HEAD_DIM, 1), lambda b, t, vp, lp: (b, 0, 0, 0))
    grid_spec = pltpu.PrefetchScalarGridSpec(
        num_scalar_prefetch=2, grid=(nb, nstep), in_specs=v_specs + p_specs + [col2], out_specs=col2,
        scratch_shapes=[pltpu.VMEM((N_HEADS, HEAD_DIM, page), F32)])
    return pl.pallas_call(_moba_gather_kernel, grid_spec=grid_spec,
                          out_shape=jax.ShapeDtypeStruct((nb, N_HEADS, HEAD_DIM, 1), F32),
                          compiler_params=_params("arbitrary", "arbitrary"), name="moba_gather")(
        vpage, lpage, *([vt_cache] * N_HEADS), *([probs] * N_HEADS), own)


def _heads(a, batch, seq):
    return a.reshape(batch, seq, N_HEADS, HEAD_DIM).transpose(0, 2, 1, 3)


def kernel(x_prompt, x_sample, cache_fox_k, cache_fox_v, cache_fox_logf, cache_moba_k, cache_moba_v, page_table,
           g_attn, w_in, b_forget, g_q_fox, g_k_fox, g_q_moba, g_k_moba, t5_table, w_branch_fox, w_branch_moba,
           w_out, g_ffn, w_router_group, w_router_expert, w_e_gate, w_e_up, w_e_down):
    depth = w_in.shape[0]
    assert depth == 1, "single-layer trunk"
    batch, seq, _ = x_prompt.shape
    nb, nt, _ = x_sample.shape
    assert nt == 1 and seq % (2 * MOBA_BLOCK) == 0
    n_p, n_s = batch * seq, nb * nt

    w = w_in[0]
    f0 = 3 * W_HEADS
    w_main = jnp.concatenate([w[:, :f0], w[:, f0 + N_HEADS:]], axis=1).astype(BF16)
    w_f = jnp.pad(w[:, f0:f0 + N_HEADS], ((0, 0), (0, LANES - N_HEADS))).astype(BF16)
    b_f = jnp.pad(b_forget[0], (0, LANES - N_HEADS))[None]
    gains = jnp.stack([jnp.tile(g[0], N_HEADS) for g in (g_q_fox, g_k_fox, g_q_moba, g_k_moba)])
    gains = jnp.pad(gains, ((0, 4), (0, 0)))
    seg = (jnp.arange(W_HEADS)[:, None] // HEAD_DIM == jnp.arange(W_HEADS)[None] // HEAD_DIM)
    seg = (seg.astype(F32) / HEAD_DIM).astype(BF16)
    wbf, wbm, wout = w_branch_fox[0].astype(BF16), w_branch_moba[0].astype(BF16), w_out[0].astype(BF16)
    w_router = jnp.concatenate([w_router_group[0], w_router_expert[0].reshape(D_MODEL, N_EXPERTS)], axis=1)
    w_router = jnp.pad(w_router, ((0, 0), (0, LANES - N_GROUPS - N_EXPERTS)))
    wg, wu, wd = w_e_gate[0].astype(BF16), w_e_up[0].astype(BF16), w_e_down[0].astype(BF16)
    proj_w = (g_attn, w_main, w_f, b_f, gains, seg)

    xp = x_prompt.reshape(n_p, D_MODEL)
    (q_f, kt_f, v_f, kt_f32, vt_f32, lf, q_m, q32_m, kt_m, v_m, kt_m32, vt_m32, kmean, sga, sgb) = \
        _project_prompt(xp, batch, seq, MOBA_BLOCK, *proj_w)
    cum = _cumsum(lf, batch, seq).reshape(batch, seq, LANES)
    rows3 = lambda a: a.reshape(batch, seq, W_HEADS)
    oa = _fox_prompt(q_f, cum, kt_f, _fox_key_rows(cum[:, :, :N_HEADS]), rows3(v_f))
    member, km, own_bias, prev_delta, prev_tile = _moba_side_operands(kmean, _moba_bias(t5_table), t5_table,
                                                                      batch, seq)
    ob = _moba_prompt(t5_table, q_m, q32_m, kt_m, member, rows3(v_m), km, own_bias, prev_delta, prev_tile)
    merge_w = (wbf, wbm, wout, g_ffn, w_router)
    x1, h2, route = _merge(xp, oa.reshape(n_p, W_HEADS), ob.reshape(n_p, W_HEADS), sga, sgb, *merge_w,
                           tm=MOBA_BLOCK)
    y_prompt = _moe(x1, h2, route, wg, wu, wd, tm=min(1024, n_p)).reshape(batch, seq, D_MODEL)

    xs = x_sample.reshape(n_s, D_MODEL)
    qa_s, ka_s, va_s, lf_s, qb_s, kb_s, vb_s, sga_s, sgb_s, _ = _project(xs, n_s, *proj_w)
    tokh = lambda a: a.reshape(n_s, N_HEADS, HEAD_DIM, 1)
    rows_last = lambda c: c[0].transpose(0, 2, 3, 1)
    oa_s = _fox_decode(page_table, tokh(qa_s), tokh(ka_s), tokh(va_s), lf_s[:, :N_HEADS, None],
                       rows_last(cache_fox_k), rows_last(cache_fox_v), cache_fox_logf[0].transpose(0, 2, 1))
    ob_s = _moba_decode(page_table, tokh(qb_s), tokh(kb_s), tokh(vb_s), t5_table[:, :, None],
                        rows_last(cache_moba_k), rows_last(cache_moba_v))
    flat = lambda o: o.reshape(n_s, W_HEADS).astype(BF16)
    x1_s, h2_s, route_s = _merge(xs, flat(oa_s), flat(ob_s), sga_s, sgb_s, *merge_w, tm=n_s)
    y_sample = _moe(x1_s, h2_s, route_s, wg, wu, wd, tm=n_s).reshape(nb, nt, D_MODEL)

    kv_p = lambda a: a.transpose(0, 3, 1, 2)[None]
    kv_s = lambda a: a.reshape(depth, nb, nt, N_HEADS, HEAD_DIM)
    return (y_prompt, y_sample,
            kv_p(kt_f32), kv_p(vt_f32), lf[:, :N_HEADS].reshape(depth, batch, seq, N_HEADS),
            kv_p(kt_m32), kv_p(vt_m32),
            kv_s(ka_s), kv_s(va_s), lf_s[:, :N_HEADS].reshape(depth, nb, nt, N_HEADS), kv_s(kb_s), kv_s(vb_s))
```

```python
import functools
import math

import numpy as np
import jax
import jax.numpy as jnp
from jax import lax
from jax.experimental import pallas as pl
from jax.experimental.pallas import tpu as pltpu

F32 = jnp.float32
BF16 = jnp.bfloat16
HIGHEST = lax.Precision.HIGHEST

D_MODEL = 1024
N_HEADS = 8
HEAD_DIM = 64
W_HEADS = N_HEADS * HEAD_DIM
MOBA_BLOCK = 256
MOBA_TOPK = 3
T5_BUCKETS = 32
T5_MAX_DIST = 128
N_GROUPS = 4
EXPERTS_PER_GROUP = 8
N_EXPERTS = N_GROUPS * EXPERTS_PER_GROUP
D_EXPERT = 256
EPS = 1e-6
NEG = -1e30
SCALE = HEAD_DIM ** -0.5
LOG2E = 1.4426950408889634
LANES = 128
ATTN_TILE = 2 * MOBA_BLOCK
VMEM_LIMIT = 56 * 1024 * 1024


def _t5_thresholds():
    max_exact = T5_BUCKETS // 2
    n = np.arange(0, 4 * T5_MAX_DIST)
    nf = np.maximum(n, max_exact).astype(np.float32)
    large = max_exact + (np.log(nf / np.float32(max_exact)) / np.float32(math.log(T5_MAX_DIST / max_exact))
                         * np.float32(T5_BUCKETS - max_exact)).astype(np.int32)
    bucket = np.where(n < max_exact, n, np.minimum(large, T5_BUCKETS - 1))
    return [int(np.argmax(bucket >= j)) for j in range(1, T5_BUCKETS)]


T5_THRESHOLDS = _t5_thresholds()


def _params(*sem):
    return pltpu.CompilerParams(dimension_semantics=sem, vmem_limit_bytes=VMEM_LIMIT)


def _proj_body(x_ref, g_ref, w_ref, wf_ref, bf_ref, gains_ref, seg_ref, precise_norm=False):
    x = x_ref[...]
    h = x * lax.rsqrt(jnp.mean(x * x, axis=-1, keepdims=True) + EPS) * g_ref[...]
    hb = h.astype(BF16)
    seg = seg_ref[...]

    def proj(i, width=W_HEADS):
        return jnp.dot(hb, w_ref[:, i:i + width], preferred_element_type=F32)

    def headnorm(z, row):
        if precise_norm:
            ms = jnp.dot(z * z, seg.astype(F32), preferred_element_type=F32, precision=HIGHEST)
        else:
            ms = jnp.dot((z * z).astype(BF16), seg, preferred_element_type=F32)
        return z * lax.rsqrt(ms + EPS) * gains_ref[row:row + 1, :]

    f = jnp.dot(hb, wf_ref[...], preferred_element_type=F32) + bf_ref[...]
    return dict(
        qa=headnorm(proj(0), 0), ka=headnorm(proj(W_HEADS), 1), va=proj(2 * W_HEADS),
        qb=headnorm(proj(3 * W_HEADS), 2), kb=headnorm(proj(4 * W_HEADS), 3), vb=proj(5 * W_HEADS),
        sga=jax.nn.sigmoid(proj(6 * W_HEADS, D_MODEL)),
        sgb=jax.nn.sigmoid(proj(6 * W_HEADS + D_MODEL, D_MODEL)),
        lf=jnp.minimum(f, 0.0) - jnp.log1p(jnp.exp(-jnp.abs(f))))


def _proj_kernel(x_ref, g_ref, w_ref, wf_ref, bf_ref, gains_ref, seg_ref,
                 qa_ref, ka_ref, va_ref, lf_ref, qb_ref, kb_ref, vb_ref, sga_ref, sgb_ref, km_ref):
    z = _proj_body(x_ref, g_ref, w_ref, wf_ref, bf_ref, gains_ref, seg_ref, precise_norm=True)
    for name, ref in (("qa", qa_ref), ("ka", ka_ref), ("va", va_ref), ("lf", lf_ref), ("qb", qb_ref),
                      ("kb", kb_ref), ("vb", vb_ref), ("sga", sga_ref), ("sgb", sgb_ref)):
        ref[...] = z[name]
    km_ref[...] = jnp.mean(z["kb"], axis=0, keepdims=True)


def _proj_prompt_kernel(x_ref, g_ref, w_ref, wf_ref, bf_ref, gains_ref, seg_ref,
                        qf_ref, ktf_ref, vf_ref, ktf32_ref, vtf32_ref, lf_ref,
                        qm_ref, qm32_ref, ktm_ref, vm_ref, ktm32_ref, vtm32_ref, km_ref, sga_ref, sgb_ref):
    z = _proj_body(x_ref, g_ref, w_ref, wf_ref, bf_ref, gains_ref, seg_ref)
    tm = x_ref.shape[0]
    lane = lax.broadcasted_iota(jnp.int32, (tm, LANES), 1)
    low = lane < HEAD_DIM
    ones = jnp.where(jnp.logical_and(lane >= HEAD_DIM, lane < HEAD_DIM + 3), 1.0, 0.0)

    def heads_low(a):
        out = []
        for p in range(N_HEADS // 2):
            pair = a[:, p * LANES:(p + 1) * LANES]
            out += [pair, pltpu.roll(pair, HEAD_DIM, 1)]
        return out

    def heads_t(a):
        return a.T.reshape(N_HEADS, HEAD_DIM, tm)

    for hh, q in enumerate(heads_low(z["qa"] * (SCALE * LOG2E))):
        qf_ref[hh] = jnp.where(low, q, ones).astype(BF16)
    for hh, q in enumerate(heads_low(z["qb"])):
        qm32_ref[hh] = q[:, :HEAD_DIM]
        qm_ref[hh] = jnp.where(low, q * (SCALE * LOG2E), 0.0).astype(BF16)
    kat, kbt = heads_t(z["ka"]), heads_t(z["kb"])
    ktf32_ref[...] = kat
    ktf_ref[...] = kat.astype(BF16)
    ktm32_ref[...] = kbt
    ktm_ref[...] = kbt.astype(BF16)
    vtf32_ref[...] = heads_t(z["va"])
    vtm32_ref[...] = heads_t(z["vb"])
    vf_ref[...] = z["va"].astype(BF16)
    vm_ref[...] = z["vb"].astype(BF16)
    km_ref[...] = jnp.mean(z["kb"], axis=0, keepdims=True)
    lf_ref[...] = z["lf"]
    sga_ref[...] = z["sga"].astype(BF16)
    sgb_ref[...] = z["sgb"].astype(BF16)


def _project_prompt(x2d, batch, seq, tm, g_attn, w_main, w_f, b_f, gains, seg):
    n = x2d.shape[0]
    tpb = seq // tm
    row = lambda i: (i, 0)
    const = lambda i: (0, 0)
    blk = lambda w: pl.BlockSpec((tm, w), row)
    qspec = lambda w: pl.BlockSpec((None, N_HEADS, tm, w), lambda i: (i // tpb, 0, i % tpb, 0))
    tspec = pl.BlockSpec((None, N_HEADS, HEAD_DIM, tm), lambda i: (i // tpb, 0, 0, i % tpb))
    hm = lambda w, dt: jax.ShapeDtypeStruct((batch, N_HEADS, seq, w), dt)
    tr = lambda dt: jax.ShapeDtypeStruct((batch, N_HEADS, HEAD_DIM, seq), dt)
    wide = lambda w, dt: jax.ShapeDtypeStruct((n, w), dt)
    out_shape = (hm(LANES, BF16), tr(BF16), wide(W_HEADS, BF16), tr(F32), tr(F32), wide(LANES, F32),
                 hm(LANES, BF16), hm(HEAD_DIM, F32), tr(BF16), wide(W_HEADS, BF16), tr(F32), tr(F32),
                 jax.ShapeDtypeStruct((n // tm, 1, W_HEADS), F32), wide(D_MODEL, BF16), wide(D_MODEL, BF16))
    out_specs = (qspec(LANES), tspec, blk(W_HEADS), tspec, tspec, blk(LANES),
                 qspec(LANES), qspec(HEAD_DIM), tspec, blk(W_HEADS), tspec, tspec,
                 pl.BlockSpec((None, 1, W_HEADS), lambda i: (i, 0, 0)), blk(D_MODEL), blk(D_MODEL))
    in_specs = [blk(D_MODEL), pl.BlockSpec((1, D_MODEL), const), pl.BlockSpec(w_main.shape, const),
                pl.BlockSpec(w_f.shape, const), pl.BlockSpec((1, LANES), const),
                pl.BlockSpec(gains.shape, const), pl.BlockSpec(seg.shape, const)]
    return pl.pallas_call(_proj_prompt_kernel, grid=(n // tm,), in_specs=in_specs, out_specs=out_specs,
                          out_shape=out_shape, compiler_params=_params("arbitrary"), name="proj_prompt")(
        x2d, g_attn, w_main, w_f, b_f, gains, seg)


def _project(x2d, tm, g_attn, w_main, w_f, b_f, gains, seg):
    n = x2d.shape[0]
    row = lambda i: (i, 0)
    const = lambda i: (0, 0)
    wide = lambda w, dt: jax.ShapeDtypeStruct((n, w), dt)
    out_shape = (wide(W_HEADS, F32), wide(W_HEADS, F32), wide(W_HEADS, F32), wide(LANES, F32),
                 wide(W_HEADS, F32), wide(W_HEADS, F32), wide(W_HEADS, F32),
                 wide(D_MODEL, F32), wide(D_MODEL, F32),
                 jax.ShapeDtypeStruct((n // tm, 1, W_HEADS), F32))
    blk = lambda w: pl.BlockSpec((tm, w), row)
    out_specs = (blk(W_HEADS), blk(W_HEADS), blk(W_HEADS), blk(LANES), blk(W_HEADS), blk(W_HEADS), blk(W_HEADS),
                 blk(D_MODEL), blk(D_MODEL), pl.BlockSpec((None, 1, W_HEADS), lambda i: (i, 0, 0)))
    in_specs = [blk(D_MODEL), pl.BlockSpec((1, D_MODEL), const), pl.BlockSpec(w_main.shape, const),
                pl.BlockSpec(w_f.shape, const), pl.BlockSpec((1, LANES), const),
                pl.BlockSpec(gains.shape, const), pl.BlockSpec(seg.shape, const)]
    return pl.pallas_call(_proj_kernel, grid=(n // tm,), in_specs=in_specs, out_specs=out_specs,
                          out_shape=out_shape, compiler_params=_params("arbitrary"), name="proj")(
        x2d, g_attn, w_main, w_f, b_f, gains, seg)


def _cumsum_kernel(lf_ref, o_ref, carry_ref, *, tc):
    @pl.when(pl.program_id(1) == 0)
    def _():
        carry_ref[...] = jnp.zeros_like(carry_ref)

    r = lax.broadcasted_iota(jnp.int32, (tc, tc), 0)
    c = lax.broadcasted_iota(jnp.int32, (tc, tc), 1)
    tri = jnp.where(c <= r, 1.0, 0.0).astype(F32)
    out = jnp.dot(tri, lf_ref[...], preferred_element_type=F32, precision=HIGHEST) + carry_ref[...]
    o_ref[...] = out
    carry_ref[...] = out[tc - 1:tc, :]


def _cumsum(lf, batch, seq, tc=256):
    nchunk = seq // tc
    spec = pl.BlockSpec((tc, LANES), lambda b, i: (b * nchunk + i, 0))
    return pl.pallas_call(functools.partial(_cumsum_kernel, tc=tc), grid=(batch, nchunk),
                          in_specs=[spec], out_specs=spec,
                          out_shape=jax.ShapeDtypeStruct(lf.shape, F32),
                          scratch_shapes=[pltpu.VMEM((1, LANES), F32)],
                          compiler_params=_params("arbitrary", "arbitrary"), name="logf_cumsum")(lf)


def _online_update(carry, s2, v):
    m, l, acc = carry
    m_new = jnp.maximum(m, jnp.max(s2, axis=-1, keepdims=True))
    alpha = jnp.exp2(m - m_new)
    p = jnp.exp2(s2 - m_new)
    l = alpha * l + jnp.sum(p, axis=-1, keepdims=True)
    acc = alpha * acc + jnp.dot(p.astype(BF16), v, preferred_element_type=F32)
    return m_new, l, acc


def _softmax_init(rows, width=LANES):
    return (jnp.full((rows, 1), NEG, F32), jnp.zeros((rows, 1), F32), jnp.zeros((rows, width), F32))


def _pair_output(results):
    (_, l0, a0), (_, l1, a1) = results
    lane = lax.broadcasted_iota(jnp.int32, a0.shape, 1)
    return jnp.where(lane < HEAD_DIM, a0 / l0, a1 / l1)


def _split3(c):
    def top(x):
        bits = lax.bitcast_convert_type(x, jnp.uint32) & jnp.uint32(0xFFFF0000)
        return lax.bitcast_convert_type(bits, F32)
    hi = top(c)
    mid = top(c - hi)
    return hi, mid, c - hi - mid


FOX_EXTRA_ROWS = 16


def _fox_kernel(q_ref, cum_ref, kt_ref, kx_ref, v_ref, o_ref, *, tq):
    pair, qi = pl.program_id(1), pl.program_id(2)
    lane = lax.broadcasted_iota(jnp.int32, (tq, LANES), 1)
    row = lax.broadcasted_iota(jnp.int32, (tq, tq), 0)
    col = lax.broadcasted_iota(jnp.int32, (tq, tq), 1)
    cum = cum_ref[...] * LOG2E
    results = []
    for hh in range(2):
        cq = jnp.sum(jnp.where(lane == 2 * pair + hh, cum, 0.0), axis=-1, keepdims=True)
        q = q_ref[hh].astype(F32)
        for piece, value in enumerate(_split3(cq)):
            q = jnp.where(lane == HEAD_DIM + 3 + piece, value, q)
        q = q.astype(BF16)

        def tile(k0, width=tq, q=q, hh=hh):
            k0 = pl.multiple_of(k0, tq)
            kt = jnp.concatenate([kt_ref[hh, :, pl.ds(k0, width)], kx_ref[hh, :, pl.ds(k0, width)],
                                  jnp.zeros((LANES - HEAD_DIM - FOX_EXTRA_ROWS, width), BF16)], axis=0)
            return jnp.dot(q, kt, preferred_element_type=F32), v_ref[pl.ds(k0, width), :]

        carry = lax.fori_loop(0, qi // 2, lambda c, carry, tile=tile: _online_update(carry, *tile(c * 2 * tq, 2 * tq)),
                              _softmax_init(tq))
        carry = lax.fori_loop(qi - qi % 2, qi, lambda c, carry, tile=tile: _online_update(carry, *tile(c * tq)), carry)
        s2, v = tile(qi * tq)
        results.append(_online_update(carry, jnp.where(col <= row, s2, NEG), v))
    o_ref[...] = _pair_output(results).astype(o_ref.dtype)


def _fox_prompt(q_aug, cum, kt, kx, v, tq=ATTN_TILE):
    b, h, s, _ = q_aug.shape
    pair = lambda shape, imap: pl.BlockSpec((None, 2) + shape, imap)
    in_specs = [pair((tq, LANES), lambda b, p, i: (b, p, i, 0)),
                pl.BlockSpec((None, tq, LANES), lambda b, p, i: (b, i, 0)),
                pair((HEAD_DIM, s), lambda b, p, i: (b, p, 0, 0)),
                pair((FOX_EXTRA_ROWS, s), lambda b, p, i: (b, p, 0, 0)),
                pl.BlockSpec((None, s, LANES), lambda b, p, i: (b, 0, p))]
    out_specs = pl.BlockSpec((None, tq, LANES), lambda b, p, i: (b, i, p))
    return pl.pallas_call(functools.partial(_fox_kernel, tq=tq), grid=(b, h // 2, s // tq), in_specs=in_specs,
                          out_specs=out_specs, out_shape=jax.ShapeDtypeStruct((b, s, h * HEAD_DIM), BF16),
                          compiler_params=_params("arbitrary", "arbitrary", "arbitrary"), name="fox_prompt")(
        q_aug, cum, kt, kx, v)


def _fox_key_rows(cum):
    pieces = _split3((cum * LOG2E).transpose(0, 2, 1))
    ones = jnp.ones_like(pieces[0])
    zero = jnp.zeros_like(pieces[0])
    return jnp.stack([-p for p in pieces] + [ones] * 3 + [zero] * (FOX_EXTRA_ROWS - 6), axis=2).astype(BF16)


def _t5_select(dist, table_at):
    bias = table_at(0)
    for j, thr in enumerate(T5_THRESHOLDS, start=1):
        bias = jnp.where(dist >= thr, table_at(j), bias)
    return bias


def _bias_kernel(tab_ref, o_ref):
    hh = pl.program_id(0)
    i = lax.broadcasted_iota(jnp.int32, (MOBA_BLOCK, 2 * MOBA_BLOCK), 0)
    j = lax.broadcasted_iota(jnp.int32, (MOBA_BLOCK, 2 * MOBA_BLOCK), 1)
    dist = MOBA_BLOCK + i - j
    o_ref[...] = _t5_select(dist, lambda bkt: tab_ref[bkt, hh])


def _moba_bias(t5_table):
    return pl.pallas_call(_bias_kernel, grid=(N_HEADS,),
                          in_specs=[pl.BlockSpec(memory_space=pltpu.SMEM)],
                          out_specs=pl.BlockSpec((None, MOBA_BLOCK, 2 * MOBA_BLOCK), lambda h: (h, 0, 0)),
                          out_shape=jax.ShapeDtypeStruct((N_HEADS, MOBA_BLOCK, 2 * MOBA_BLOCK), F32),
                          compiler_params=_params("arbitrary"), name="moba_bias")(t5_table)


MOBA_MAX_BLOCKS = 32
SEL_LANE0 = HEAD_DIM
SEL_LANE1 = HEAD_DIM + MOBA_MAX_BLOCKS


def _moba_kernel(tab_ref, qp_ref, q32_ref, kt_ref, member_ref, v_ref, km_ref, own_ref, prev_ref, prevtile_ref,
                 o_ref, qsel_ref):
    blk, tq = MOBA_BLOCK, 2 * MOBA_BLOCK
    pair, i = pl.program_id(1), pl.program_id(2)
    nblk = MOBA_MAX_BLOCKS
    dot = functools.partial(jnp.dot, preferred_element_type=F32)
    blk_i = lax.broadcasted_iota(jnp.int32, (nblk, tq), 0)
    blk_f = blk_i.astype(F32)
    own = 2 * i + (lax.broadcasted_iota(jnp.int32, (nblk, tq), 1) >= blk).astype(jnp.int32)
    lane = lax.broadcasted_iota(jnp.int32, (tq, LANES), 1)
    causal = lax.broadcasted_iota(jnp.int32, (blk, blk), 1) <= lax.broadcasted_iota(jnp.int32, (blk, blk), 0)
    k0 = pl.multiple_of(i * tq, tq)
    v_d = v_ref[pl.ds(k0, tq), :]
    results = []
    for hh in range(2):
        qp = qp_ref[hh]
        gate = lax.dot_general(km_ref[hh].astype(BF16), q32_ref[hh].astype(BF16), (((1,), (1,)), ((), ())),
                               preferred_element_type=F32)
        g = jnp.where(blk_i < own, gate, -jnp.inf)
        chosen = jnp.zeros((nblk, tq), F32)
        for _ in range(MOBA_TOPK):
            mx = jnp.max(g, axis=0, keepdims=True)
            idx = jnp.min(jnp.where(g == mx, blk_f, float(nblk)), axis=0, keepdims=True)
            pick = jnp.logical_and(blk_f == idx, mx > -jnp.inf)
            chosen = jnp.where(pick, 1.0, chosen)
            g = jnp.where(pick, -jnp.inf, g)
        far = jnp.full((1, 1), tab_ref[T5_BUCKETS - 1, 2 * pair + hh] * LOG2E, F32)
        far_hi = far.astype(BF16).astype(F32)
        picked = chosen > 0.5
        sel_t = jnp.concatenate([jnp.zeros((SEL_LANE0, tq), F32), jnp.where(picked, far_hi, NEG),
                                 jnp.where(picked, far - far_hi, 0.0)], axis=0)
        qsel_ref[hh] = jnp.where(lane >= SEL_LANE0, sel_t.T.astype(BF16), qp)

        def keys(k0, width=tq, hh=hh):
            return jnp.concatenate([kt_ref[hh, :, pl.ds(k0, width)], member_ref[:, pl.ds(k0, width)]], axis=0)

        kt_d = keys(k0)
        own_bias = own_ref[hh]
        s_a = jnp.where(causal, dot(qp[:blk], kt_d[:, :blk]) + own_bias, NEG)
        s_b = dot(qsel_ref[hh, blk:, :], kt_d[:, :blk]) + prev_ref[hh]
        s_c = jnp.where(causal, dot(qp[blk:], kt_d[:, blk:]) + own_bias, NEG)
        s_d = jnp.concatenate([jnp.concatenate([s_a, jnp.full((blk, blk), NEG, F32)], axis=1),
                               jnp.concatenate([s_b, s_c], axis=1)], axis=0)
        carry = _online_update(_softmax_init(tq), s_d, v_d)

        def tile(c, carry, near=False, tiles=1, hh=hh, keys=keys):
            k0 = pl.multiple_of(c * tiles * tq, tq)
            s2 = dot(qsel_ref[hh], keys(k0, tiles * tq))
            if near:
                s2 = s2 + prevtile_ref[hh]
            return _online_update(carry, s2, v_ref[pl.ds(k0, tiles * tq), :])

        n_far = jnp.maximum(i - 1, 0)
        carry = lax.fori_loop(n_far, i, functools.partial(tile, near=True), carry)
        carry = lax.fori_loop(0, n_far // 2, functools.partial(tile, tiles=2), carry)
        results.append(lax.fori_loop(n_far - n_far % 2, n_far, tile, carry))
    o_ref[...] = _pair_output(results).astype(o_ref.dtype)


def _moba_prompt(t5_table, q_pad, q32, kt, member, v, km, own_bias, prev_delta, prev_tile):
    b, h, s, _ = q_pad.shape
    blk, tq = MOBA_BLOCK, 2 * MOBA_BLOCK
    pair = lambda shape, imap: pl.BlockSpec((None, 2) + shape, imap)
    per_pair = lambda shape: pl.BlockSpec((2,) + shape, lambda b, p, i: (p, 0, 0))
    in_specs = [pl.BlockSpec(memory_space=pltpu.SMEM),
                pair((tq, LANES), lambda b, p, i: (b, p, i, 0)),
                pair((tq, HEAD_DIM), lambda b, p, i: (b, p, i, 0)),
                pair((HEAD_DIM, s), lambda b, p, i: (b, p, 0, 0)),
                pl.BlockSpec((2 * MOBA_MAX_BLOCKS, s), lambda b, p, i: (0, 0)),
                pl.BlockSpec((None, s, LANES), lambda b, p, i: (b, 0, p)),
                pair((MOBA_MAX_BLOCKS, HEAD_DIM), lambda b, p, i: (b, p, 0, 0)),
                per_pair((blk, blk)), per_pair((blk, blk)), per_pair((tq, tq))]
    out_specs = pl.BlockSpec((None, tq, LANES), lambda b, p, i: (b, i, p))
    return pl.pallas_call(_moba_kernel, grid=(b, h // 2, s // tq), in_specs=in_specs, out_specs=out_specs,
                          out_shape=jax.ShapeDtypeStruct((b, s, h * HEAD_DIM), BF16),
                          scratch_shapes=[pltpu.VMEM((2, tq, LANES), BF16)],
                          compiler_params=_params("arbitrary", "arbitrary", "arbitrary"), name="moba_prompt")(
        t5_table, q_pad, q32, kt, member, v, km, own_bias, prev_delta, prev_tile)


def _moba_side_operands(kmean, bias_tiles, t5_table, batch, seq):
    blk = MOBA_BLOCK
    nb = seq // blk
    assert nb <= MOBA_MAX_BLOCKS
    member = (jnp.arange(MOBA_MAX_BLOCKS)[:, None] == jnp.arange(seq)[None] // blk).astype(BF16)
    member = jnp.concatenate([member, member], axis=0)
    km = kmean.reshape(batch, nb, N_HEADS, HEAD_DIM).transpose(0, 2, 1, 3)
    km = jnp.pad(km, ((0, 0), (0, 0), (0, MOBA_MAX_BLOCKS - nb), (0, 0)))
    far = t5_table[T5_BUCKETS - 1][:, None, None]
    own_bias = bias_tiles[:, :, blk:] * LOG2E
    prev_delta = (bias_tiles[:, :, :blk] - far) * LOG2E
    prev_tile = jnp.zeros((N_HEADS, 2 * blk, 2 * blk), F32).at[:, :blk, blk:].set(prev_delta)
    return member, km, own_bias, prev_delta, prev_tile


def _merge_kernel(x_ref, oa_ref, ob_ref, sga_ref, sgb_ref, wbf_ref, wbm_ref, wout_ref, g_ref, wr_ref,
                  x1_ref, h2_ref, route_ref):
    ma = jnp.dot(oa_ref[...], wbf_ref[...], preferred_element_type=F32)
    mb = jnp.dot(ob_ref[...], wbm_ref[...], preferred_element_type=F32)
    mix = sga_ref[...].astype(F32) * ma + sgb_ref[...].astype(F32) * mb
    x1 = x_ref[...] + jnp.dot(mix.astype(BF16), wout_ref[...], preferred_element_type=F32)
    x1_ref[...] = x1
    h2 = x1 * lax.rsqrt(jnp.mean(x1 * x1, axis=-1, keepdims=True) + EPS) * g_ref[...]
    h2_ref[...] = h2.astype(BF16)

    logits = jnp.dot(h2.astype(BF16), wr_ref[...].astype(BF16), preferred_element_type=F32)
    lane = lax.broadcasted_iota(jnp.int32, logits.shape, 1)
    big = LANES

    def argmax_low(vals):
        mx = jnp.max(vals, axis=-1, keepdims=True)
        return mx, jnp.min(jnp.where(vals == mx, lane, big), axis=-1, keepdims=True)

    glog = jnp.where(lane < N_GROUPS, logits, -jnp.inf)
    gmax, gsel = argmax_low(glog)
    gprob = 1.0 / jnp.sum(jnp.exp(glog - gmax), axis=-1, keepdims=True)
    first = N_GROUPS + gsel * EXPERTS_PER_GROUP
    in_group = jnp.logical_and(lane >= first, lane < first + EXPERTS_PER_GROUP)
    elog = jnp.where(in_group, logits, -jnp.inf)
    v0, i0 = argmax_low(elog)
    v1, i1 = argmax_low(jnp.where(lane == i0, -jnp.inf, elog))
    e1 = jnp.exp(v1 - v0)
    w0 = gprob / (1.0 + e1)
    w1 = gprob * e1 / (1.0 + e1)
    route = jnp.where(lane == 0, (i0 - N_GROUPS).astype(F32),
                      jnp.where(lane == 1, (i1 - N_GROUPS).astype(F32),
                                jnp.where(lane == 2, w0, jnp.where(lane == 3, w1, 0.0))))
    route_ref[...] = route


def _merge(x2d, oa, ob, sga, sgb, wbf, wbm, wout, g_ffn, w_router, tm):
    n = x2d.shape[0]
    row = lambda i: (i, 0)
    const = lambda i: (0, 0)
    blk = lambda w: pl.BlockSpec((tm, w), row)
    full = lambda a: pl.BlockSpec(a.shape, const)
    in_specs = [blk(D_MODEL), blk(W_HEADS), blk(W_HEADS), blk(D_MODEL), blk(D_MODEL),
                full(wbf), full(wbm), full(wout), full(g_ffn), full(w_router)]
    out_shape = (jax.ShapeDtypeStruct((n, D_MODEL), F32), jax.ShapeDtypeStruct((n, D_MODEL), BF16),
                 jax.ShapeDtypeStruct((n, LANES), F32))
    return pl.pallas_call(_merge_kernel, grid=(n // tm,), in_specs=in_specs,
                          out_specs=(blk(D_MODEL), blk(D_MODEL), blk(LANES)), out_shape=out_shape,
                          compiler_params=_params("arbitrary"), name="merge_router")(
        x2d, oa, ob, sga, sgb, wbf, wbm, wout, g_ffn, w_router)


def _moe_kernel(x1_ref, h2_ref, route_ref, wg_ref, wu_ref, wd_ref, o_ref):
    e = pl.program_id(1)

    @pl.when(e == 0)
    def _():
        o_ref[...] = x1_ref[...]

    h = h2_ref[...]
    a = jax.nn.silu(jnp.dot(h, wg_ref[...], preferred_element_type=F32)) * \
        jnp.dot(h, wu_ref[...], preferred_element_type=F32)
    y = jnp.dot(a.astype(BF16), wd_ref[...], preferred_element_type=F32)
    r = route_ref[...]
    ef = e.astype(F32)
    gate = jnp.where(r[:, 0:1] == ef, r[:, 2:3], 0.0) + jnp.where(r[:, 1:2] == ef, r[:, 3:4], 0.0)
    o_ref[...] += y * gate


def _moe(x1, h2, route, wg, wu, wd, tm):
    n = x1.shape[0]
    row = lambda i, e: (i, 0)
    in_specs = [pl.BlockSpec((tm, D_MODEL), row), pl.BlockSpec((tm, D_MODEL), row), pl.BlockSpec((tm, LANES), row),
                pl.BlockSpec((None, D_MODEL, D_EXPERT), lambda i, e: (e, 0, 0)),
                pl.BlockSpec((None, D_MODEL, D_EXPERT), lambda i, e: (e, 0, 0)),
                pl.BlockSpec((None, D_EXPERT, D_MODEL), lambda i, e: (e, 0, 0))]
    return pl.pallas_call(_moe_kernel, grid=(n // tm, N_EXPERTS), in_specs=in_specs,
                          out_specs=pl.BlockSpec((tm, D_MODEL), row),
                          out_shape=jax.ShapeDtypeStruct((n, D_MODEL), F32),
                          compiler_params=_params("arbitrary", "arbitrary"), name="experts")(
        x1, h2, route, wg, wu, wd)


DECODE_PAGES_PER_STEP = 8


def _as_mxu(x):
    return x.astype(BF16).astype(F32)


def _head_blocks(rows):
    hh = lax.broadcasted_iota(jnp.int32, (rows, W_HEADS), 0)
    ll = lax.broadcasted_iota(jnp.int32, (rows, W_HEADS), 1)
    return ll // HEAD_DIM == hh


def _fox_decode_kernel(pt_ref, q_ref, kn_ref, vn_ref, lfn_ref, *rest, page, group):
    k_refs, v_refs, lf_refs = rest[:group], rest[group:2 * group], rest[2 * group:3 * group]
    o_ref, s_ref, m_ref, l_ref, acc_ref, carry_ref = rest[3 * group:]
    phase, j = pl.program_id(1), pl.program_id(2)
    blocks = _head_blocks(N_HEADS)
    q_rows = jnp.where(blocks, _as_mxu(q_ref[...]), 0.0)
    s_new = jnp.sum(q_rows * _as_mxu(kn_ref[...]), axis=-1, keepdims=True) * SCALE

    @pl.when(phase == 0)
    def _():
        @pl.when(j == 0)
        def _():
            m_ref[...] = s_new
            carry_ref[...] = lfn_ref[...]

        r0 = lax.broadcasted_iota(jnp.int32, (page, page), 0)
        r1 = lax.broadcasted_iota(jnp.int32, (page, page), 1)
        later = jnp.where(r0 > r1, 1.0, 0.0).astype(F32)
        q_b = q_rows.astype(BF16)
        carry = carry_ref[...]
        m = m_ref[...]
        for g in range(group):
            lf = lf_refs[g][...]
            decay = jnp.dot(lf, later, preferred_element_type=F32, precision=HIGHEST) + carry
            keys = k_refs[g][...].reshape(W_HEADS, page).astype(BF16)
            s = jnp.dot(q_b, keys, preferred_element_type=F32) * SCALE + decay
            s_ref[pl.ds(j * group + g, 1)] = s[None]
            m = jnp.maximum(m, jnp.max(s, axis=-1, keepdims=True))
            carry = carry + jnp.sum(lf, axis=-1, keepdims=True)
        carry_ref[...] = carry
        m_ref[...] = m

    @pl.when(phase == 1)
    def _():
        m = m_ref[...]

        @pl.when(j == 0)
        def _():
            e = jnp.exp(s_ref[...] - m[None])
            l = jnp.sum(jnp.sum(e, axis=0), axis=-1, keepdims=True) + jnp.exp(s_new - m)
            l_ref[...] = l
            acc_ref[...] = _as_mxu(jnp.exp(s_new - m) / l) * _as_mxu(vn_ref[...])

        l = l_ref[...]
        acc = acc_ref[...]
        for g in range(group):
            p = (jnp.exp(s_ref[j * group + g] - m) / l).astype(BF16)
            values = v_refs[g][...].reshape(W_HEADS, page).astype(BF16)
            acc = acc + lax.dot_general(p, values, (((1,), (1,)), ((), ())), preferred_element_type=F32)
        acc_ref[...] = acc

        @pl.when(j == pl.num_programs(2) - 1)
        def _():
            o_ref[...] = jnp.sum(jnp.where(blocks, acc_ref[...], 0.0), axis=0, keepdims=True)


def _fox_decode(page_table, q, kn, vn, lfn, kt_cache, vt_cache, lft_cache):
    nb, n_pages = page_table.shape
    page = kt_cache.shape[-1]
    group = math.gcd(DECODE_PAGES_PER_STEP, n_pages)
    steps = n_pages // group
    tok3 = lambda b, ph, j, pt: (b, 0, 0)

    def paged(g, ndim, key_side):
        def index(b, ph, j, pt):
            step = j * (1 - ph) + (steps - 1) * ph if key_side else j * ph
            return (pt[b, n_pages - 1 - (step * group + g)],) + (0,) * ndim
        return index

    row = pl.BlockSpec((None, 1, W_HEADS), tok3)
    k_specs = [pl.BlockSpec((None, N_HEADS, HEAD_DIM, page), paged(g, 3, True)) for g in range(group)]
    v_specs = [pl.BlockSpec((None, N_HEADS, HEAD_DIM, page), paged(g, 3, False)) for g in range(group)]
    lf_specs = [pl.BlockSpec((None, N_HEADS, page), paged(g, 2, True)) for g in range(group)]
    in_specs = [row, row, row, pl.BlockSpec((None, N_HEADS, 1), tok3)] + k_specs + v_specs + lf_specs
    grid_spec = pltpu.PrefetchScalarGridSpec(
        num_scalar_prefetch=1, grid=(nb, 2, steps), in_specs=in_specs, out_specs=row,
        scratch_shapes=[pltpu.VMEM((n_pages, N_HEADS, page), F32),
                        pltpu.VMEM((N_HEADS, 1), F32), pltpu.VMEM((N_HEADS, 1), F32),
                        pltpu.VMEM((N_HEADS, W_HEADS), F32), pltpu.VMEM((N_HEADS, 1), F32)])
    return pl.pallas_call(functools.partial(_fox_decode_kernel, page=page, group=group), grid_spec=grid_spec,
                          out_shape=jax.ShapeDtypeStruct((nb, 1, W_HEADS), F32),
                          compiler_params=_params("arbitrary", "arbitrary", "arbitrary"), name="fox_decode")(
        page_table, q, kn, vn, lfn, *([kt_cache] * group), *([vt_cache] * group), *([lft_cache] * group))


def _moba_score_kernel(pt_ref, q_ref, qrow_ref, kn_ref, vn_ref, tab_ref, *rest, page, n_pages, group):
    k_refs = rest[:group]
    p_ref, sel_ref, own_ref, s_ref, gate_ref = rest[group:]
    j = pl.program_id(1)
    ppb = MOBA_BLOCK // page
    nblk = n_pages // ppb
    p_len = n_pages * page
    q3 = _as_mxu(q_ref[...])
    q_b = jnp.where(_head_blocks(N_HEADS), qrow_ref[...], 0.0).astype(BF16)
    for g in range(group):
        keys = k_refs[g][...].reshape(W_HEADS, page).astype(BF16)
        s_ref[pl.ds(j * group + g, 1)] = jnp.dot(q_b, keys, preferred_element_type=F32)[None]
    for g0 in range(0, group, ppb):
        kblk = k_refs[g0][...]
        for t in range(1, ppb):
            kblk = kblk + k_refs[g0 + t][...]
        kmean = jnp.sum(kblk, axis=-1, keepdims=True) * (1.0 / MOBA_BLOCK)
        gate_ref[pl.ds((j * group + g0) // ppb, 1)] = jnp.sum(q3 * _as_mxu(kmean), axis=1)[None]

    @pl.when(j == pl.num_programs(1) - 1)
    def _():
        s = s_ref[...].reshape(nblk, ppb, N_HEADS, page)
        gate = gate_ref[...]
        idx = lax.broadcasted_iota(jnp.int32, gate.shape, 0)
        lane = lax.broadcasted_iota(jnp.int32, (N_HEADS, LANES), 1)
        chosen = jnp.zeros(gate.shape, F32)
        picks = jnp.zeros((N_HEADS, LANES), jnp.int32)
        for r in range(min(MOBA_TOPK, nblk)):
            mx = jnp.max(gate, axis=0, keepdims=True)
            first = jnp.min(jnp.where(gate == mx, idx, nblk), axis=0, keepdims=True)
            pick = idx == first
            chosen = jnp.where(pick, 1.0, chosen)
            gate = jnp.where(pick, -jnp.inf, gate)
            picks = jnp.where(lane == r, first[0], picks)
        sel_ref[...] = picks

        logits = s * SCALE + tab_ref[T5_BUCKETS - 1][None, None]
        masked = jnp.where(chosen[:, None] > 0.5, logits, NEG).reshape(n_pages, N_HEADS, page)
        s_ref[...] = masked
        for pg in range(n_pages):
            if p_len - (pg * page + page - 1) < T5_THRESHOLDS[-1]:
                dist = p_len - pg * page - lax.broadcasted_iota(jnp.int32, (1, page), 1)
                bias = _t5_select(dist, lambda bkt: tab_ref[bkt])
                near = s[pg // ppb, pg % ppb] * SCALE + bias
                s_ref[pg] = jnp.where(chosen[pg // ppb] > 0.5, near, NEG)
        masked = s_ref[...]
        s_own = jnp.sum(q3 * _as_mxu(kn_ref[...]), axis=1) * SCALE + tab_ref[0]
        m = jnp.maximum(jnp.max(jnp.max(masked, axis=0), axis=-1, keepdims=True), s_own)
        p = jnp.exp(masked - m[None])
        p_own = jnp.exp(s_own - m)
        l = jnp.sum(jnp.sum(p, axis=0), axis=-1, keepdims=True) + p_own
        p_ref[...] = _as_mxu(p / l[None])
        own_ref[...] = _as_mxu(p_own / l)[:, :, None] * _as_mxu(vn_ref[...])


def _moba_gather_kernel(vpage_ref, lpage_ref, *rest):
    v_refs, p_refs = rest[:N_HEADS], rest[N_HEADS:2 * N_HEADS]
    own_ref, o_ref, acc_ref = rest[2 * N_HEADS:]
    t = pl.program_id(1)

    @pl.when(t == 0)
    def _():
        acc_ref[...] = jnp.zeros_like(acc_ref)

    for h in range(N_HEADS):
        acc_ref[h] += _as_mxu(v_refs[h][...]) * p_refs[h][h:h + 1, :]

    @pl.when(t == pl.num_programs(1) - 1)
    def _():
        o_ref[...] = jnp.sum(acc_ref[...], axis=-1, keepdims=True) + own_ref[...]


def _moba_decode(page_table, q, qrow, kn, vn, tab3, kt_cache, vt_cache):
    nb, n_pages = page_table.shape
    page = kt_cache.shape[-1]
    ppb = MOBA_BLOCK // page
    nblk = n_pages // ppb
    ntop = min(MOBA_TOPK, nblk)
    group = math.gcd(DECODE_PAGES_PER_STEP, n_pages)
    assert group % ppb == 0, "a grid step must hold whole MoBA blocks"
    tok4 = lambda b, j, pt: (b, 0, 0, 0)
    col = pl.BlockSpec((None, N_HEADS, HEAD_DIM, 1), tok4)
    k_specs = [pl.BlockSpec((None, N_HEADS, HEAD_DIM, page),
                            (lambda g: lambda b, j, pt: (pt[b, j * group + g], 0, 0, 0))(g)) for g in range(group)]
    in_specs = [col, pl.BlockSpec((None, 1, W_HEADS), lambda b, j, pt: (b, 0, 0)), col, col,
                pl.BlockSpec((T5_BUCKETS, N_HEADS, 1), lambda b, j, pt: (0, 0, 0))] + k_specs
    out_specs = (pl.BlockSpec((None, n_pages, N_HEADS, page), tok4),
                 pl.BlockSpec((None, N_HEADS, LANES), lambda b, j, pt: (b, 0, 0)), col)
    out_shape = (jax.ShapeDtypeStruct((nb, n_pages, N_HEADS, page), F32),
                 jax.ShapeDtypeStruct((nb, N_HEADS, LANES), jnp.int32),
                 jax.ShapeDtypeStruct((nb, N_HEADS, HEAD_DIM, 1), F32))
    grid_spec = pltpu.PrefetchScalarGridSpec(
        num_scalar_prefetch=1, grid=(nb, n_pages // group), in_specs=in_specs, out_specs=out_specs,
        scratch_shapes=[pltpu.VMEM((n_pages, N_HEADS, page), F32), pltpu.VMEM((nblk, N_HEADS, 1), F32)])
    probs, picks, own = pl.pallas_call(
        functools.partial(_moba_score_kernel, page=page, n_pages=n_pages, group=group), grid_spec=grid_spec,
        out_shape=out_shape, compiler_params=_params("arbitrary", "arbitrary"), name="moba_score")(
        page_table, q, qrow, kn, vn, tab3, *([kt_cache] * group))

    lpage = (picks[:, :, :ntop, None] * ppb + jnp.arange(ppb)[None, None, None]).reshape(nb, N_HEADS, ntop * ppb)
    vpage = jnp.take_along_axis(page_table[:, None, :], lpage, axis=2)
    nstep = ntop * ppb
    lpage, vpage = lpage.reshape(nb, N_HEADS * nstep), vpage.reshape(nb, N_HEADS * nstep)
    v_specs = [pl.BlockSpec((None, None, HEAD_DIM, page),
                            (lambda h: lambda b, t, vp, lp: (vp[b, h * nstep + t], h, 0, 0))(h))
               for h in range(N_HEADS)]
    p_specs = [pl.BlockSpec((None, None, N_HEADS, page),
                            (lambda h: lambda b, t, vp, lp: (b, lp[b, h * nstep + t], 0, 0))(h))
               for h in range(N_HEADS)]
    col2 = pl.BlockSpec((None, N_HEADS, HEAD_DIM, 1), lambda b, t, vp, lp: (b, 0, 0, 0))
    grid_spec = pltpu.PrefetchScalarGridSpec(
        num_scalar_prefetch=2, grid=(nb, nstep), in_specs=v_specs + p_specs + [col2], out_specs=col2,
        scratch_shapes=[pltpu.VMEM((N_HEADS, HEAD_DIM, page), F32)])
    return pl.pallas_call(_moba_gather_kernel, grid_spec=grid_spec,
                          out_shape=jax.ShapeDtypeStruct((nb, N_HEADS, HEAD_DIM, 1), F32),
                          compiler_params=_params("arbitrary", "arbitrary"), name="moba_gather")(
        vpage, lpage, *([vt_cache] * N_HEADS), *([probs] * N_HEADS), own)


def _heads(a, batch, seq):
    return a.reshape(batch, seq, N_HEADS, HEAD_DIM).transpose(0, 2, 1, 3)


def kernel(x_prompt, x_sample, cache_fox_k, cache_fox_v, cache_fox_logf, cache_moba_k, cache_moba_v, page_table,
           g_attn, w_in, b_forget, g_q_fox, g_k_fox, g_q_moba, g_k_moba, t5_table, w_branch_fox, w_branch_moba,
           w_out, g_ffn, w_router_group, w_router_expert, w_e_gate, w_e_up, w_e_down):
    depth = w_in.shape[0]
    assert depth == 1, "single-layer trunk"
    batch, seq, _ = x_prompt.shape
    nb, nt, _ = x_sample.shape
    assert nt == 1 and seq % (2 * MOBA_BLOCK) == 0
    n_p, n_s = batch * seq, nb * nt

    w = w_in[0]
    f0 = 3 * W_HEADS
    w_main = jnp.concatenate([w[:, :f0], w[:, f0 + N_HEADS:]], axis=1).astype(BF16)
    w_f = jnp.pad(w[:, f0:f0 + N_HEADS], ((0, 0), (0, LANES - N_HEADS))).astype(BF16)
    b_f = jnp.pad(b_forget[0], (0, LANES - N_HEADS))[None]
    gains = jnp.stack([jnp.tile(g[0], N_HEADS) for g in (g_q_fox, g_k_fox, g_q_moba, g_k_moba)])
    gains = jnp.pad(gains, ((0, 4), (0, 0)))
    seg = (jnp.arange(W_HEADS)[:, None] // HEAD_DIM == jnp.arange(W_HEADS)[None] // HEAD_DIM)
    seg = (seg.astype(F32) / HEAD_DIM).astype(BF16)
    wbf, wbm, wout = w_branch_fox[0].astype(BF16), w_branch_moba[0].astype(BF16), w_out[0].astype(BF16)
    w_router = jnp.concatenate([w_router_group[0], w_router_expert[0].reshape(D_MODEL, N_EXPERTS)], axis=1)
    w_router = jnp.pad(w_router, ((0, 0), (0, LANES - N_GROUPS - N_EXPERTS)))
    wg, wu, wd = w_e_gate[0].astype(BF16), w_e_up[0].astype(BF16), w_e_down[0].astype(BF16)
    proj_w = (g_attn, w_main, w_f, b_f, gains, seg)

    xp = x_prompt.reshape(n_p, D_MODEL)
    (q_f, kt_f, v_f, kt_f32, vt_f32, lf, q_m, q32_m, kt_m, v_m, kt_m32, vt_m32, kmean, sga, sgb) = \
        _project_prompt(xp, batch, seq, MOBA_BLOCK, *proj_w)
    cum = _cumsum(lf, batch, seq).reshape(batch, seq, LANES)
    rows3 = lambda a: a.reshape(batch, seq, W_HEADS)
    oa = _fox_prompt(q_f, cum, kt_f, _fox_key_rows(cum[:, :, :N_HEADS]), rows3(v_f))
    member, km, own_bias, prev_delta, prev_tile = _moba_side_operands(kmean, _moba_bias(t5_table), t5_table,
                                                                      batch, seq)
    ob = _moba_prompt(t5_table, q_m, q32_m, kt_m, member, rows3(v_m), km, own_bias, prev_delta, prev_tile)
    merge_w = (wbf, wbm, wout, g_ffn, w_router)
    x1, h2, route = _merge(xp, oa.reshape(n_p, W_HEADS), ob.reshape(n_p, W_HEADS), sga, sgb, *merge_w,
                           tm=MOBA_BLOCK)
    y_prompt = _moe(x1, h2, route, wg, wu, wd, tm=min(1024, n_p)).reshape(batch, seq, D_MODEL)

    xs = x_sample.reshape(n_s, D_MODEL)
    qa_s, ka_s, va_s, lf_s, qb_s, kb_s, vb_s, sga_s, sgb_s, _ = _project(xs, n_s, *proj_w)
    tokh = lambda a: a.reshape(n_s, N_HEADS, HEAD_DIM, 1)
    rows_last = lambda c: c[0].transpose(0, 2, 3, 1)
    tokrow = lambda a: a.reshape(n_s, 1, W_HEADS)
    oa_s = _fox_decode(page_table, tokrow(qa_s), tokrow(ka_s), tokrow(va_s), lf_s[:, :N_HEADS, None],
                       rows_last(cache_fox_k), rows_last(cache_fox_v), cache_fox_logf[0].transpose(0, 2, 1))
    ob_s = _moba_decode(page_table, tokh(qb_s), tokrow(qb_s), tokh(kb_s), tokh(vb_s), t5_table[:, :, None],
                        rows_last(cache_moba_k), rows_last(cache_moba_v))
    flat = lambda o: o.reshape(n_s, W_HEADS).astype(BF16)
    x1_s, h2_s, route_s = _merge(xs, flat(oa_s), flat(ob_s), sga_s, sgb_s, *merge_w, tm=n_s)
    y_sample = _moe(x1_s, h2_s, route_s, wg, wu, wd, tm=n_s).reshape(nb, nt, D_MODEL)

    kv_p = lambda a: a.transpose(0, 3, 1, 2)[None]
    kv_s = lambda a: a.reshape(depth, nb, nt, N_HEADS, HEAD_DIM)
    return (y_prompt, y_sample,
            kv_p(kt_f32), kv_p(vt_f32), lf[:, :N_HEADS].reshape(depth, batch, seq, N_HEADS),
            kv_p(kt_m32), kv_p(vt_m32),
            kv_s(ka_s), kv_s(va_s), lf_s[:, :N_HEADS].reshape(depth, nb, nt, N_HEADS), kv_s(kb_s), kv_s(vb_s))
```

```python
import functools
import math

import numpy as np
import jax
import jax.numpy as jnp
from jax import lax
from jax.experimental import pallas as pl
from jax.experimental.pallas import tpu as pltpu

F32 = jnp.float32
BF16 = jnp.bfloat16
HIGHEST = lax.Precision.HIGHEST

D_MODEL = 1024
N_HEADS = 8
HEAD_DIM = 64
W_HEADS = N_HEADS * HEAD_DIM
MOBA_BLOCK = 256
MOBA_TOPK = 3
T5_BUCKETS = 32
T5_MAX_DIST = 128
N_GROUPS = 4
EXPERTS_PER_GROUP = 8
N_EXPERTS = N_GROUPS * EXPERTS_PER_GROUP
D_EXPERT = 256
EPS = 1e-6
NEG = -1e30
SCALE = HEAD_DIM ** -0.5
LOG2E = 1.4426950408889634
LANES = 128
ATTN_TILE = 2 * MOBA_BLOCK
VMEM_LIMIT = 56 * 1024 * 1024


def _t5_thresholds():
    max_exact = T5_BUCKETS // 2
    n = np.arange(0, 4 * T5_MAX_DIST)
    nf = np.maximum(n, max_exact).astype(np.float32)
    large = max_exact + (np.log(nf / np.float32(max_exact)) / np.float32(math.log(T5_MAX_DIST / max_exact))
                         * np.float32(T5_BUCKETS - max_exact)).astype(np.int32)
    bucket = np.where(n < max_exact, n, np.minimum(large, T5_BUCKETS - 1))
    return [int(np.argmax(bucket >= j)) for j in range(1, T5_BUCKETS)]


T5_THRESHOLDS = _t5_thresholds()


def _params(*sem):
    return pltpu.CompilerParams(dimension_semantics=sem, vmem_limit_bytes=VMEM_LIMIT)


def _proj_body(x_ref, g_ref, w_ref, wf_ref, bf_ref, gains_ref, seg_ref, precise_norm=False):
    x = x_ref[...]
    h = x * lax.rsqrt(jnp.mean(x * x, axis=-1, keepdims=True) + EPS) * g_ref[...]
    hb = h.astype(BF16)
    seg = seg_ref[...]

    def proj(i, width=W_HEADS):
        return jnp.dot(hb, w_ref[:, i:i + width], preferred_element_type=F32)

    def headnorm(z, row):
        if precise_norm:
            ms = jnp.dot(z * z, seg.astype(F32), preferred_element_type=F32, precision=HIGHEST)
        else:
            ms = jnp.dot((z * z).astype(BF16), seg, preferred_element_type=F32)
        return z * lax.rsqrt(ms + EPS) * gains_ref[row:row + 1, :]

    f = jnp.dot(hb, wf_ref[...], preferred_element_type=F32) + bf_ref[...]
    return dict(
        qa=headnorm(proj(0), 0), ka=headnorm(proj(W_HEADS), 1), va=proj(2 * W_HEADS),
        qb=headnorm(proj(3 * W_HEADS), 2), kb=headnorm(proj(4 * W_HEADS), 3), vb=proj(5 * W_HEADS),
        sga=jax.nn.sigmoid(proj(6 * W_HEADS, D_MODEL)),
        sgb=jax.nn.sigmoid(proj(6 * W_HEADS + D_MODEL, D_MODEL)),
        lf=jnp.minimum(f, 0.0) - jnp.log1p(jnp.exp(-jnp.abs(f))))


def _proj_kernel(x_ref, g_ref, w_ref, wf_ref, bf_ref, gains_ref, seg_ref,
                 qa_ref, ka_ref, va_ref, lf_ref, qb_ref, kb_ref, vb_ref, sga_ref, sgb_ref, km_ref):
    z = _proj_body(x_ref, g_ref, w_ref, wf_ref, bf_ref, gains_ref, seg_ref, precise_norm=True)
    for name, ref in (("qa", qa_ref), ("ka", ka_ref), ("va", va_ref), ("lf", lf_ref), ("qb", qb_ref),
                      ("kb", kb_ref), ("vb", vb_ref), ("sga", sga_ref), ("sgb", sgb_ref)):
        ref[...] = z[name]
    km_ref[...] = jnp.mean(z["kb"], axis=0, keepdims=True)


def _proj_prompt_kernel(x_ref, g_ref, w_ref, wf_ref, bf_ref, gains_ref, seg_ref,
                        qf_ref, ktf_ref, vf_ref, ktf32_ref, vtf32_ref, lf_ref,
                        qm_ref, qm32_ref, ktm_ref, vm_ref, ktm32_ref, vtm32_ref, km_ref, sga_ref, sgb_ref):
    z = _proj_body(x_ref, g_ref, w_ref, wf_ref, bf_ref, gains_ref, seg_ref)
    tm = x_ref.shape[0]
    lane = lax.broadcasted_iota(jnp.int32, (tm, LANES), 1)
    low = lane < HEAD_DIM
    ones = jnp.where(jnp.logical_and(lane >= HEAD_DIM, lane < HEAD_DIM + 3), 1.0, 0.0)

    def heads_low(a):
        out = []
        for p in range(N_HEADS // 2):
            pair = a[:, p * LANES:(p + 1) * LANES]
            out += [pair, pltpu.roll(pair, HEAD_DIM, 1)]
        return out

    def heads_t(a):
        return a.T.reshape(N_HEADS, HEAD_DIM, tm)

    for hh, q in enumerate(heads_low(z["qa"] * (SCALE * LOG2E))):
        qf_ref[hh] = jnp.where(low, q, ones).astype(BF16)
    for hh, q in enumerate(heads_low(z["qb"])):
        qm32_ref[hh] = q[:, :HEAD_DIM]
        qm_ref[hh] = jnp.where(low, q * (SCALE * LOG2E), 0.0).astype(BF16)
    kat, kbt = heads_t(z["ka"]), heads_t(z["kb"])
    ktf32_ref[...] = kat
    ktf_ref[...] = kat.astype(BF16)
    ktm32_ref[...] = kbt
    ktm_ref[...] = kbt.astype(BF16)
    vtf32_ref[...] = heads_t(z["va"])
    vtm32_ref[...] = heads_t(z["vb"])
    vf_ref[...] = z["va"].astype(BF16)
    vm_ref[...] = z["vb"].astype(BF16)
    km_ref[...] = jnp.mean(z["kb"], axis=0, keepdims=True)
    lf_ref[...] = z["lf"]
    sga_ref[...] = z["sga"].astype(BF16)
    sgb_ref[...] = z["sgb"].astype(BF16)


def _project_prompt(x2d, batch, seq, tm, g_attn, w_main, w_f, b_f, gains, seg):
    n = x2d.shape[0]
    tpb = seq // tm
    row = lambda i: (i, 0)
    const = lambda i: (0, 0)
    blk = lambda w: pl.BlockSpec((tm, w), row)
    qspec = lambda w: pl.BlockSpec((None, N_HEADS, tm, w), lambda i: (i // tpb, 0, i % tpb, 0))
    tspec = pl.BlockSpec((None, N_HEADS, HEAD_DIM, tm), lambda i: (i // tpb, 0, 0, i % tpb))
    hm = lambda w, dt: jax.ShapeDtypeStruct((batch, N_HEADS, seq, w), dt)
    tr = lambda dt: jax.ShapeDtypeStruct((batch, N_HEADS, HEAD_DIM, seq), dt)
    wide = lambda w, dt: jax.ShapeDtypeStruct((n, w), dt)
    out_shape = (hm(LANES, BF16), tr(BF16), wide(W_HEADS, BF16), tr(F32), tr(F32), wide(LANES, F32),
                 hm(LANES, BF16), hm(HEAD_DIM, F32), tr(BF16), wide(W_HEADS, BF16), tr(F32), tr(F32),
                 jax.ShapeDtypeStruct((n // tm, 1, W_HEADS), F32), wide(D_MODEL, BF16), wide(D_MODEL, BF16))
    out_specs = (qspec(LANES), tspec, blk(W_HEADS), tspec, tspec, blk(LANES),
                 qspec(LANES), qspec(HEAD_DIM), tspec, blk(W_HEADS), tspec, tspec,
                 pl.BlockSpec((None, 1, W_HEADS), lambda i: (i, 0, 0)), blk(D_MODEL), blk(D_MODEL))
    in_specs = [blk(D_MODEL), pl.BlockSpec((1, D_MODEL), const), pl.BlockSpec(w_main.shape, const),
                pl.BlockSpec(w_f.shape, const), pl.BlockSpec((1, LANES), const),
                pl.BlockSpec(gains.shape, const), pl.BlockSpec(seg.shape, const)]
    return pl.pallas_call(_proj_prompt_kernel, grid=(n // tm,), in_specs=in_specs, out_specs=out_specs,
                          out_shape=out_shape, compiler_params=_params("arbitrary"), name="proj_prompt")(
        x2d, g_attn, w_main, w_f, b_f, gains, seg)


def _project(x2d, tm, g_attn, w_main, w_f, b_f, gains, seg):
    n = x2d.shape[0]
    row = lambda i: (i, 0)
    const = lambda i: (0, 0)
    wide = lambda w, dt: jax.ShapeDtypeStruct((n, w), dt)
    out_shape = (wide(W_HEADS, F32), wide(W_HEADS, F32), wide(W_HEADS, F32), wide(LANES, F32),
                 wide(W_HEADS, F32), wide(W_HEADS, F32), wide(W_HEADS, F32),
                 wide(D_MODEL, F32), wide(D_MODEL, F32),
                 jax.ShapeDtypeStruct((n // tm, 1, W_HEADS), F32))
    blk = lambda w: pl.BlockSpec((tm, w), row)
    out_specs = (blk(W_HEADS), blk(W_HEADS), blk(W_HEADS), blk(LANES), blk(W_HEADS), blk(W_HEADS), blk(W_HEADS),
                 blk(D_MODEL), blk(D_MODEL), pl.BlockSpec((None, 1, W_HEADS), lambda i: (i, 0, 0)))
    in_specs = [blk(D_MODEL), pl.BlockSpec((1, D_MODEL), const), pl.BlockSpec(w_main.shape, const),
                pl.BlockSpec(w_f.shape, const), pl.BlockSpec((1, LANES), const),
                pl.BlockSpec(gains.shape, const), pl.BlockSpec(seg.shape, const)]
    return pl.pallas_call(_proj_kernel, grid=(n // tm,), in_specs=in_specs, out_specs=out_specs,
                          out_shape=out_shape, compiler_params=_params("arbitrary"), name="proj")(
        x2d, g_attn, w_main, w_f, b_f, gains, seg)


def _cumsum_kernel(lf_ref, o_ref, carry_ref, *, tc):
    @pl.when(pl.program_id(1) == 0)
    def _():
        carry_ref[...] = jnp.zeros_like(carry_ref)

    r = lax.broadcasted_iota(jnp.int32, (tc, tc), 0)
    c = lax.broadcasted_iota(jnp.int32, (tc, tc), 1)
    tri = jnp.where(c <= r, 1.0, 0.0).astype(F32)
    out = jnp.dot(tri, lf_ref[...], preferred_element_type=F32, precision=HIGHEST) + carry_ref[...]
    o_ref[...] = out
    carry_ref[...] = out[tc - 1:tc, :]


def _cumsum(lf, batch, seq, tc=256):
    nchunk = seq // tc
    spec = pl.BlockSpec((tc, LANES), lambda b, i: (b * nchunk + i, 0))
    return pl.pallas_call(functools.partial(_cumsum_kernel, tc=tc), grid=(batch, nchunk),
                          in_specs=[spec], out_specs=spec,
                          out_shape=jax.ShapeDtypeStruct(lf.shape, F32),
                          scratch_shapes=[pltpu.VMEM((1, LANES), F32)],
                          compiler_params=_params("arbitrary", "arbitrary"), name="logf_cumsum")(lf)


def _online_update(carry, s2, v):
    m, l, acc = carry
    m_new = jnp.maximum(m, jnp.max(s2, axis=-1, keepdims=True))
    alpha = jnp.exp2(m - m_new)
    p = jnp.exp2(s2 - m_new)
    l = alpha * l + jnp.sum(p, axis=-1, keepdims=True)
    acc = alpha * acc + jnp.dot(p.astype(BF16), v, preferred_element_type=F32)
    return m_new, l, acc


def _softmax_init(rows, width=LANES):
    return (jnp.full((rows, 1), NEG, F32), jnp.zeros((rows, 1), F32), jnp.zeros((rows, width), F32))


def _pair_output(results):
    (_, l0, a0), (_, l1, a1) = results
    lane = lax.broadcasted_iota(jnp.int32, a0.shape, 1)
    return jnp.where(lane < HEAD_DIM, a0 / l0, a1 / l1)


def _split3(c):
    def top(x):
        bits = lax.bitcast_convert_type(x, jnp.uint32) & jnp.uint32(0xFFFF0000)
        return lax.bitcast_convert_type(bits, F32)
    hi = top(c)
    mid = top(c - hi)
    return hi, mid, c - hi - mid


FOX_EXTRA_ROWS = 16


def _fox_kernel(q_ref, cum_ref, kt_ref, kx_ref, v_ref, o_ref, *, tq):
    pair, qi = pl.program_id(1), pl.program_id(2)
    lane = lax.broadcasted_iota(jnp.int32, (tq, LANES), 1)
    row = lax.broadcasted_iota(jnp.int32, (tq, tq), 0)
    col = lax.broadcasted_iota(jnp.int32, (tq, tq), 1)
    cum = cum_ref[...] * LOG2E
    results = []
    for hh in range(2):
        cq = jnp.sum(jnp.where(lane == 2 * pair + hh, cum, 0.0), axis=-1, keepdims=True)
        q = q_ref[hh].astype(F32)
        for piece, value in enumerate(_split3(cq)):
            q = jnp.where(lane == HEAD_DIM + 3 + piece, value, q)
        q = q.astype(BF16)

        def tile(k0, width=tq, q=q, hh=hh):
            k0 = pl.multiple_of(k0, tq)
            kt = jnp.concatenate([kt_ref[hh, :, pl.ds(k0, width)], kx_ref[hh, :, pl.ds(k0, width)],
                                  jnp.zeros((LANES - HEAD_DIM - FOX_EXTRA_ROWS, width), BF16)], axis=0)
            return jnp.dot(q, kt, preferred_element_type=F32), v_ref[pl.ds(k0, width), :]

        carry = lax.fori_loop(0, qi // 2, lambda c, carry, tile=tile: _online_update(carry, *tile(c * 2 * tq, 2 * tq)),
                              _softmax_init(tq))
        carry = lax.fori_loop(qi - qi % 2, qi, lambda c, carry, tile=tile: _online_update(carry, *tile(c * tq)), carry)
        s2, v = tile(qi * tq)
        results.append(_online_update(carry, jnp.where(col <= row, s2, NEG), v))
    o_ref[...] = _pair_output(results).astype(o_ref.dtype)


def _fox_prompt(q_aug, cum, kt, kx, v, tq=ATTN_TILE):
    b, h, s, _ = q_aug.shape
    pair = lambda shape, imap: pl.BlockSpec((None, 2) + shape, imap)
    in_specs = [pair((tq, LANES), lambda b, p, i: (b, p, i, 0)),
                pl.BlockSpec((None, tq, LANES), lambda b, p, i: (b, i, 0)),
                pair((HEAD_DIM, s), lambda b, p, i: (b, p, 0, 0)),
                pair((FOX_EXTRA_ROWS, s), lambda b, p, i: (b, p, 0, 0)),
                pl.BlockSpec((None, s, LANES), lambda b, p, i: (b, 0, p))]
    out_specs = pl.BlockSpec((None, tq, LANES), lambda b, p, i: (b, i, p))
    return pl.pallas_call(functools.partial(_fox_kernel, tq=tq), grid=(b, h // 2, s // tq), in_specs=in_specs,
                          out_specs=out_specs, out_shape=jax.ShapeDtypeStruct((b, s, h * HEAD_DIM), BF16),
                          compiler_params=_params("arbitrary", "arbitrary", "arbitrary"), name="fox_prompt")(
        q_aug, cum, kt, kx, v)


def _fox_key_rows(cum):
    pieces = _split3((cum * LOG2E).transpose(0, 2, 1))
    ones = jnp.ones_like(pieces[0])
    zero = jnp.zeros_like(pieces[0])
    return jnp.stack([-p for p in pieces] + [ones] * 3 + [zero] * (FOX_EXTRA_ROWS - 6), axis=2).astype(BF16)


def _t5_select(dist, table_at):
    bias = table_at(0)
    for j, thr in enumerate(T5_THRESHOLDS, start=1):
        bias = jnp.where(dist >= thr, table_at(j), bias)
    return bias


def _bias_kernel(tab_ref, o_ref):
    hh = pl.program_id(0)
    i = lax.broadcasted_iota(jnp.int32, (MOBA_BLOCK, 2 * MOBA_BLOCK), 0)
    j = lax.broadcasted_iota(jnp.int32, (MOBA_BLOCK, 2 * MOBA_BLOCK), 1)
    dist = MOBA_BLOCK + i - j
    o_ref[...] = _t5_select(dist, lambda bkt: tab_ref[bkt, hh])


def _moba_bias(t5_table):
    return pl.pallas_call(_bias_kernel, grid=(N_HEADS,),
                          in_specs=[pl.BlockSpec(memory_space=pltpu.SMEM)],
                          out_specs=pl.BlockSpec((None, MOBA_BLOCK, 2 * MOBA_BLOCK), lambda h: (h, 0, 0)),
                          out_shape=jax.ShapeDtypeStruct((N_HEADS, MOBA_BLOCK, 2 * MOBA_BLOCK), F32),
                          compiler_params=_params("arbitrary"), name="moba_bias")(t5_table)


MOBA_MAX_BLOCKS = 32
SEL_LANE0 = HEAD_DIM
SEL_LANE1 = HEAD_DIM + MOBA_MAX_BLOCKS


def _moba_kernel(tab_ref, qp_ref, q32_ref, kt_ref, member_ref, v_ref, km_ref, own_ref, prev_ref, prevtile_ref,
                 o_ref, qsel_ref):
    blk, tq = MOBA_BLOCK, 2 * MOBA_BLOCK
    pair, i = pl.program_id(1), pl.program_id(2)
    nblk = MOBA_MAX_BLOCKS
    dot = functools.partial(jnp.dot, preferred_element_type=F32)
    blk_i = lax.broadcasted_iota(jnp.int32, (nblk, tq), 0)
    blk_f = blk_i.astype(F32)
    own = 2 * i + (lax.broadcasted_iota(jnp.int32, (nblk, tq), 1) >= blk).astype(jnp.int32)
    lane = lax.broadcasted_iota(jnp.int32, (tq, LANES), 1)
    causal = lax.broadcasted_iota(jnp.int32, (blk, blk), 1) <= lax.broadcasted_iota(jnp.int32, (blk, blk), 0)
    k0 = pl.multiple_of(i * tq, tq)
    v_d = v_ref[pl.ds(k0, tq), :]
    results = []
    for hh in range(2):
        qp = qp_ref[hh]
        gate = lax.dot_general(km_ref[hh].astype(BF16), q32_ref[hh].astype(BF16), (((1,), (1,)), ((), ())),
                               preferred_element_type=F32)
        g = jnp.where(blk_i < own, gate, -jnp.inf)
        chosen = jnp.zeros((nblk, tq), F32)
        for _ in range(MOBA_TOPK):
            mx = jnp.max(g, axis=0, keepdims=True)
            idx = jnp.min(jnp.where(g == mx, blk_f, float(nblk)), axis=0, keepdims=True)
            pick = jnp.logical_and(blk_f == idx, mx > -jnp.inf)
            chosen = jnp.where(pick, 1.0, chosen)
            g = jnp.where(pick, -jnp.inf, g)
        far = jnp.full((1, 1), tab_ref[T5_BUCKETS - 1, 2 * pair + hh] * LOG2E, F32)
        far_hi = far.astype(BF16).astype(F32)
        picked = chosen > 0.5
        sel_t = jnp.concatenate([jnp.zeros((SEL_LANE0, tq), F32), jnp.where(picked, far_hi, NEG),
                                 jnp.where(picked, far - far_hi, 0.0)], axis=0)
        qsel_ref[hh] = jnp.where(lane >= SEL_LANE0, sel_t.T.astype(BF16), qp)

        def keys(k0, width=tq, hh=hh):
            return jnp.concatenate([kt_ref[hh, :, pl.ds(k0, width)], member_ref[:, pl.ds(k0, width)]], axis=0)

        kt_d = keys(k0)
        own_bias = own_ref[hh]
        s_a = jnp.where(causal, dot(qp[:blk], kt_d[:, :blk]) + own_bias, NEG)
        s_b = dot(qsel_ref[hh, blk:, :], kt_d[:, :blk]) + prev_ref[hh]
        s_c = jnp.where(causal, dot(qp[blk:], kt_d[:, blk:]) + own_bias, NEG)
        s_d = jnp.concatenate([jnp.concatenate([s_a, jnp.full((blk, blk), NEG, F32)], axis=1),
                               jnp.concatenate([s_b, s_c], axis=1)], axis=0)
        carry = _online_update(_softmax_init(tq), s_d, v_d)

        def tile(c, carry, near=False, tiles=1, hh=hh, keys=keys):
            k0 = pl.multiple_of(c * tiles * tq, tq)
            s2 = dot(qsel_ref[hh], keys(k0, tiles * tq))
            if near:
                s2 = s2 + prevtile_ref[hh]
            return _online_update(carry, s2, v_ref[pl.ds(k0, tiles * tq), :])

        n_far = jnp.maximum(i - 1, 0)
        carry = lax.fori_loop(n_far, i, functools.partial(tile, near=True), carry)
        carry = lax.fori_loop(0, n_far // 2, functools.partial(tile, tiles=2), carry)
        results.append(lax.fori_loop(n_far - n_far % 2, n_far, tile, carry))
    o_ref[...] = _pair_output(results).astype(o_ref.dtype)


def _moba_prompt(t5_table, q_pad, q32, kt, member, v, km, own_bias, prev_delta, prev_tile):
    b, h, s, _ = q_pad.shape
    blk, tq = MOBA_BLOCK, 2 * MOBA_BLOCK
    pair = lambda shape, imap: pl.BlockSpec((None, 2) + shape, imap)
    per_pair = lambda shape: pl.BlockSpec((2,) + shape, lambda b, p, i: (p, 0, 0))
    in_specs = [pl.BlockSpec(memory_space=pltpu.SMEM),
                pair((tq, LANES), lambda b, p, i: (b, p, i, 0)),
                pair((tq, HEAD_DIM), lambda b, p, i: (b, p, i, 0)),
                pair((HEAD_DIM, s), lambda b, p, i: (b, p, 0, 0)),
                pl.BlockSpec((2 * MOBA_MAX_BLOCKS, s), lambda b, p, i: (0, 0)),
                pl.BlockSpec((None, s, LANES), lambda b, p, i: (b, 0, p)),
                pair((MOBA_MAX_BLOCKS, HEAD_DIM), lambda b, p, i: (b, p, 0, 0)),
                per_pair((blk, blk)), per_pair((blk, blk)), per_pair((tq, tq))]
    out_specs = pl.BlockSpec((None, tq, LANES), lambda b, p, i: (b, i, p))
    return pl.pallas_call(_moba_kernel, grid=(b, h // 2, s // tq), in_specs=in_specs, out_specs=out_specs,
                          out_shape=jax.ShapeDtypeStruct((b, s, h * HEAD_DIM), BF16),
                          scratch_shapes=[pltpu.VMEM((2, tq, LANES), BF16)],
                          compiler_params=_params("arbitrary", "arbitrary", "arbitrary"), name="moba_prompt")(
        t5_table, q_pad, q32, kt, member, v, km, own_bias, prev_delta, prev_tile)


def _moba_side_operands(kmean, bias_tiles, t5_table, batch, seq):
    blk = MOBA_BLOCK
    nb = seq // blk
    assert nb <= MOBA_MAX_BLOCKS
    member = (jnp.arange(MOBA_MAX_BLOCKS)[:, None] == jnp.arange(seq)[None] // blk).astype(BF16)
    member = jnp.concatenate([member, member], axis=0)
    km = kmean.reshape(batch, nb, N_HEADS, HEAD_DIM).transpose(0, 2, 1, 3)
    km = jnp.pad(km, ((0, 0), (0, 0), (0, MOBA_MAX_BLOCKS - nb), (0, 0)))
    far = t5_table[T5_BUCKETS - 1][:, None, None]
    own_bias = bias_tiles[:, :, blk:] * LOG2E
    prev_delta = (bias_tiles[:, :, :blk] - far) * LOG2E
    prev_tile = jnp.zeros((N_HEADS, 2 * blk, 2 * blk), F32).at[:, :blk, blk:].set(prev_delta)
    return member, km, own_bias, prev_delta, prev_tile


def _merge_kernel(x_ref, oa_ref, ob_ref, sga_ref, sgb_ref, wbf_ref, wbm_ref, wout_ref, g_ref, wr_ref,
                  x1_ref, h2_ref, route_ref):
    ma = jnp.dot(oa_ref[...], wbf_ref[...], preferred_element_type=F32)
    mb = jnp.dot(ob_ref[...], wbm_ref[...], preferred_element_type=F32)
    mix = sga_ref[...].astype(F32) * ma + sgb_ref[...].astype(F32) * mb
    x1 = x_ref[...] + jnp.dot(mix.astype(BF16), wout_ref[...], preferred_element_type=F32)
    x1_ref[...] = x1
    h2 = x1 * lax.rsqrt(jnp.mean(x1 * x1, axis=-1, keepdims=True) + EPS) * g_ref[...]
    h2_ref[...] = h2.astype(BF16)

    logits = jnp.dot(h2.astype(BF16), wr_ref[...].astype(BF16), preferred_element_type=F32)
    lane = lax.broadcasted_iota(jnp.int32, logits.shape, 1)
    big = LANES

    def argmax_low(vals):
        mx = jnp.max(vals, axis=-1, keepdims=True)
        return mx, jnp.min(jnp.where(vals == mx, lane, big), axis=-1, keepdims=True)

    glog = jnp.where(lane < N_GROUPS, logits, -jnp.inf)
    gmax, gsel = argmax_low(glog)
    gprob = 1.0 / jnp.sum(jnp.exp(glog - gmax), axis=-1, keepdims=True)
    first = N_GROUPS + gsel * EXPERTS_PER_GROUP
    in_group = jnp.logical_and(lane >= first, lane < first + EXPERTS_PER_GROUP)
    elog = jnp.where(in_group, logits, -jnp.inf)
    v0, i0 = argmax_low(elog)
    v1, i1 = argmax_low(jnp.where(lane == i0, -jnp.inf, elog))
    e1 = jnp.exp(v1 - v0)
    w0 = gprob / (1.0 + e1)
    w1 = gprob * e1 / (1.0 + e1)
    route = jnp.where(lane == 0, (i0 - N_GROUPS).astype(F32),
                      jnp.where(lane == 1, (i1 - N_GROUPS).astype(F32),
                                jnp.where(lane == 2, w0, jnp.where(lane == 3, w1, 0.0))))
    route_ref[...] = route


def _merge(x2d, oa, ob, sga, sgb, wbf, wbm, wout, g_ffn, w_router, tm):
    n = x2d.shape[0]
    row = lambda i: (i, 0)
    const = lambda i: (0, 0)
    blk = lambda w: pl.BlockSpec((tm, w), row)
    full = lambda a: pl.BlockSpec(a.shape, const)
    in_specs = [blk(D_MODEL), blk(W_HEADS), blk(W_HEADS), blk(D_MODEL), blk(D_MODEL),
                full(wbf), full(wbm), full(wout), full(g_ffn), full(w_router)]
    out_shape = (jax.ShapeDtypeStruct((n, D_MODEL), F32), jax.ShapeDtypeStruct((n, D_MODEL), BF16),
                 jax.ShapeDtypeStruct((n, LANES), F32))
    return pl.pallas_call(_merge_kernel, grid=(n // tm,), in_specs=in_specs,
                          out_specs=(blk(D_MODEL), blk(D_MODEL), blk(LANES)), out_shape=out_shape,
                          compiler_params=_params("arbitrary"), name="merge_router")(
        x2d, oa, ob, sga, sgb, wbf, wbm, wout, g_ffn, w_router)


def _moe_kernel(x1_ref, h2_ref, route_ref, wg_ref, wu_ref, wd_ref, o_ref):
    e = pl.program_id(1)

    @pl.when(e == 0)
    def _():
        o_ref[...] = x1_ref[...]

    h = h2_ref[...]
    a = jax.nn.silu(jnp.dot(h, wg_ref[...], preferred_element_type=F32)) * \
        jnp.dot(h, wu_ref[...], preferred_element_type=F32)
    y = jnp.dot(a.astype(BF16), wd_ref[...], preferred_element_type=F32)
    r = route_ref[...]
    ef = e.astype(F32)
    gate = jnp.where(r[:, 0:1] == ef, r[:, 2:3], 0.0) + jnp.where(r[:, 1:2] == ef, r[:, 3:4], 0.0)
    o_ref[...] += y * gate


def _moe(x1, h2, route, wg, wu, wd, tm):
    n = x1.shape[0]
    row = lambda i, e: (i, 0)
    in_specs = [pl.BlockSpec((tm, D_MODEL), row), pl.BlockSpec((tm, D_MODEL), row), pl.BlockSpec((tm, LANES), row),
                pl.BlockSpec((None, D_MODEL, D_EXPERT), lambda i, e: (e, 0, 0)),
                pl.BlockSpec((None, D_MODEL, D_EXPERT), lambda i, e: (e, 0, 0)),
                pl.BlockSpec((None, D_EXPERT, D_MODEL), lambda i, e: (e, 0, 0))]
    return pl.pallas_call(_moe_kernel, grid=(n // tm, N_EXPERTS), in_specs=in_specs,
                          out_specs=pl.BlockSpec((tm, D_MODEL), row),
                          out_shape=jax.ShapeDtypeStruct((n, D_MODEL), F32),
                          compiler_params=_params("arbitrary", "arbitrary"), name="experts")(
        x1, h2, route, wg, wu, wd)


DECODE_PAGES_PER_STEP = 16


def _as_mxu(x):
    return x.astype(BF16).astype(F32)


def _head_blocks(rows):
    hh = lax.broadcasted_iota(jnp.int32, (rows, W_HEADS), 0)
    ll = lax.broadcasted_iota(jnp.int32, (rows, W_HEADS), 1)
    return ll // HEAD_DIM == hh


def _fox_decode_kernel(pt_ref, q_ref, kn_ref, vn_ref, lfn_ref, *rest, page, group):
    k_refs, v_refs, lf_refs = rest[:group], rest[group:2 * group], rest[2 * group:3 * group]
    o_ref, s_ref, m_ref, l_ref, acc_ref, carry_ref = rest[3 * group:]
    phase, j = pl.program_id(1), pl.program_id(2)
    blocks = _head_blocks(N_HEADS)
    q_rows = jnp.where(blocks, _as_mxu(q_ref[...]), 0.0)
    s_new = jnp.sum(q_rows * _as_mxu(kn_ref[...]), axis=-1, keepdims=True) * SCALE

    @pl.when(phase == 0)
    def _():
        @pl.when(j == 0)
        def _():
            m_ref[...] = s_new
            carry_ref[...] = lfn_ref[...]

        r0 = lax.broadcasted_iota(jnp.int32, (page, page), 0)
        r1 = lax.broadcasted_iota(jnp.int32, (page, page), 1)
        later = jnp.where(r0 > r1, 1.0, 0.0).astype(F32)
        q_b = q_rows.astype(BF16)
        carry = carry_ref[...]
        m = m_ref[...]
        for g in range(group):
            lf = lf_refs[g][...]
            decay = jnp.dot(lf, later, preferred_element_type=F32, precision=HIGHEST) + carry
            keys = k_refs[g][...].reshape(W_HEADS, page).astype(BF16)
            s = jnp.dot(q_b, keys, preferred_element_type=F32) * SCALE + decay
            s_ref[pl.ds(j * group + g, 1)] = s[None]
            m = jnp.maximum(m, jnp.max(s, axis=-1, keepdims=True))
            carry = carry + jnp.sum(lf, axis=-1, keepdims=True)
        carry_ref[...] = carry
        m_ref[...] = m

    @pl.when(phase == 1)
    def _():
        m = m_ref[...]

        @pl.when(j == 0)
        def _():
            e = jnp.exp(s_ref[...] - m[None])
            l = jnp.sum(jnp.sum(e, axis=0), axis=-1, keepdims=True) + jnp.exp(s_new - m)
            l_ref[...] = l
            acc_ref[...] = _as_mxu(jnp.exp(s_new - m) / l) * _as_mxu(vn_ref[...])

        l = l_ref[...]
        acc = acc_ref[...]
        for g in range(group):
            p = (jnp.exp(s_ref[j * group + g] - m) / l).astype(BF16)
            values = v_refs[g][...].reshape(W_HEADS, page).astype(BF16)
            acc = acc + lax.dot_general(p, values, (((1,), (1,)), ((), ())), preferred_element_type=F32)
        acc_ref[...] = acc

        @pl.when(j == pl.num_programs(2) - 1)
        def _():
            o_ref[...] = jnp.sum(jnp.where(blocks, acc_ref[...], 0.0), axis=0, keepdims=True)


def _fox_decode(page_table, q, kn, vn, lfn, kt_cache, vt_cache, lft_cache):
    nb, n_pages = page_table.shape
    page = kt_cache.shape[-1]
    group = math.gcd(DECODE_PAGES_PER_STEP, n_pages)
    steps = n_pages // group
    tok3 = lambda b, ph, j, pt: (b, 0, 0)

    def paged(g, ndim, key_side):
        def index(b, ph, j, pt):
            step = j * (1 - ph) + (steps - 1) * ph if key_side else j * ph
            return (pt[b, n_pages - 1 - (step * group + g)],) + (0,) * ndim
        return index

    row = pl.BlockSpec((None, 1, W_HEADS), tok3)
    k_specs = [pl.BlockSpec((None, N_HEADS, HEAD_DIM, page), paged(g, 3, True)) for g in range(group)]
    v_specs = [pl.BlockSpec((None, N_HEADS, HEAD_DIM, page), paged(g, 3, False)) for g in range(group)]
    lf_specs = [pl.BlockSpec((None, N_HEADS, page), paged(g, 2, True)) for g in range(group)]
    in_specs = [row, row, row, pl.BlockSpec((None, N_HEADS, 1), tok3)] + k_specs + v_specs + lf_specs
    grid_spec = pltpu.PrefetchScalarGridSpec(
        num_scalar_prefetch=1, grid=(nb, 2, steps), in_specs=in_specs, out_specs=row,
        scratch_shapes=[pltpu.VMEM((n_pages, N_HEADS, page), F32),
                        pltpu.VMEM((N_HEADS, 1), F32), pltpu.VMEM((N_HEADS, 1), F32),
                        pltpu.VMEM((N_HEADS, W_HEADS), F32), pltpu.VMEM((N_HEADS, 1), F32)])
    return pl.pallas_call(functools.partial(_fox_decode_kernel, page=page, group=group), grid_spec=grid_spec,
                          out_shape=jax.ShapeDtypeStruct((nb, 1, W_HEADS), F32),
                          compiler_params=_params("arbitrary", "arbitrary", "arbitrary"), name="fox_decode")(
        page_table, q, kn, vn, lfn, *([kt_cache] * group), *([vt_cache] * group), *([lft_cache] * group))


def _moba_score_kernel(pt_ref, q_ref, qrow_ref, kn_ref, vn_ref, tab_ref, *rest, page, n_pages, group):
    k_refs = rest[:group]
    p_ref, sel_ref, own_ref, s_ref, gate_ref = rest[group:]
    j = pl.program_id(1)
    ppb = MOBA_BLOCK // page
    nblk = n_pages // ppb
    p_len = n_pages * page
    q3 = _as_mxu(q_ref[...])
    q_b = jnp.where(_head_blocks(N_HEADS), qrow_ref[...], 0.0).astype(BF16)
    for g in range(group):
        keys = k_refs[g][...].reshape(W_HEADS, page).astype(BF16)
        s_ref[pl.ds(j * group + g, 1)] = jnp.dot(q_b, keys, preferred_element_type=F32)[None]
    for g0 in range(0, group, ppb):
        kblk = k_refs[g0][...]
        for t in range(1, ppb):
            kblk = kblk + k_refs[g0 + t][...]
        kmean = jnp.sum(kblk, axis=-1, keepdims=True) * (1.0 / MOBA_BLOCK)
        gate_ref[pl.ds((j * group + g0) // ppb, 1)] = jnp.sum(q3 * _as_mxu(kmean), axis=1)[None]

    @pl.when(j == pl.num_programs(1) - 1)
    def _():
        s = s_ref[...].reshape(nblk, ppb, N_HEADS, page)
        gate = gate_ref[...]
        idx = lax.broadcasted_iota(jnp.int32, gate.shape, 0)
        lane = lax.broadcasted_iota(jnp.int32, (N_HEADS, LANES), 1)
        chosen = jnp.zeros(gate.shape, F32)
        picks = jnp.zeros((N_HEADS, LANES), jnp.int32)
        for r in range(min(MOBA_TOPK, nblk)):
            mx = jnp.max(gate, axis=0, keepdims=True)
            first = jnp.min(jnp.where(gate == mx, idx, nblk), axis=0, keepdims=True)
            pick = idx == first
            chosen = jnp.where(pick, 1.0, chosen)
            gate = jnp.where(pick, -jnp.inf, gate)
            picks = jnp.where(lane == r, first[0], picks)
        sel_ref[...] = picks

        logits = s * SCALE + tab_ref[T5_BUCKETS - 1][None, None]
        masked = jnp.where(chosen[:, None] > 0.5, logits, NEG).reshape(n_pages, N_HEADS, page)
        s_ref[...] = masked
        for pg in range(n_pages):
            if p_len - (pg * page + page - 1) < T5_THRESHOLDS[-1]:
                dist = p_len - pg * page - lax.broadcasted_iota(jnp.int32, (1, page), 1)
                bias = _t5_select(dist, lambda bkt: tab_ref[bkt])
                near = s[pg // ppb, pg % ppb] * SCALE + bias
                s_ref[pg] = jnp.where(chosen[pg // ppb] > 0.5, near, NEG)
        masked = s_ref[...]
        s_own = jnp.sum(q3 * _as_mxu(kn_ref[...]), axis=1) * SCALE + tab_ref[0]
        m = jnp.maximum(jnp.max(jnp.max(masked, axis=0), axis=-1, keepdims=True), s_own)
        p = jnp.exp(masked - m[None])
        p_own = jnp.exp(s_own - m)
        l = jnp.sum(jnp.sum(p, axis=0), axis=-1, keepdims=True) + p_own
        p_ref[...] = _as_mxu(p / l[None])
        own_ref[...] = _as_mxu(p_own / l)[:, :, None] * _as_mxu(vn_ref[...])


def _moba_gather_kernel(vpage_ref, lpage_ref, *rest):
    v_refs, p_refs = rest[:N_HEADS], rest[N_HEADS:2 * N_HEADS]
    own_ref, o_ref, acc_ref = rest[2 * N_HEADS:]
    t = pl.program_id(1)

    @pl.when(t == 0)
    def _():
        acc_ref[...] = jnp.zeros_like(acc_ref)

    for h in range(N_HEADS):
        acc_ref[h] += _as_mxu(v_refs[h][...]) * p_refs[h][h:h + 1, :]

    @pl.when(t == pl.num_programs(1) - 1)
    def _():
        o_ref[...] = jnp.sum(acc_ref[...], axis=-1, keepdims=True) + own_ref[...]


def _moba_decode(page_table, q, qrow, kn, vn, tab3, kt_cache, vt_cache):
    nb, n_pages = page_table.shape
    page = kt_cache.shape[-1]
    ppb = MOBA_BLOCK // page
    nblk = n_pages // ppb
    ntop = min(MOBA_TOPK, nblk)
    group = math.gcd(DECODE_PAGES_PER_STEP, n_pages)
    assert group % ppb == 0, "a grid step must hold whole MoBA blocks"
    tok4 = lambda b, j, pt: (b, 0, 0, 0)
    col = pl.BlockSpec((None, N_HEADS, HEAD_DIM, 1), tok4)
    k_specs = [pl.BlockSpec((None, N_HEADS, HEAD_DIM, page),
                            (lambda g: lambda b, j, pt: (pt[b, j * group + g], 0, 0, 0))(g)) for g in range(group)]
    in_specs = [col, pl.BlockSpec((None, 1, W_HEADS), lambda b, j, pt: (b, 0, 0)), col, col,
                pl.BlockSpec((T5_BUCKETS, N_HEADS, 1), lambda b, j, pt: (0, 0, 0))] + k_specs
    out_specs = (pl.BlockSpec((None, n_pages, N_HEADS, page), tok4),
                 pl.BlockSpec((None, N_HEADS, LANES), lambda b, j, pt: (b, 0, 0)), col)
    out_shape = (jax.ShapeDtypeStruct((nb, n_pages, N_HEADS, page), F32),
                 jax.ShapeDtypeStruct((nb, N_HEADS, LANES), jnp.int32),
                 jax.ShapeDtypeStruct((nb, N_HEADS, HEAD_DIM, 1), F32))
    grid_spec = pltpu.PrefetchScalarGridSpec(
        num_scalar_prefetch=1, grid=(nb, n_pages // group), in_specs=in_specs, out_specs=out_specs,
        scratch_shapes=[pltpu.VMEM((n_pages, N_HEADS, page), F32), pltpu.VMEM((nblk, N_HEADS, 1), F32)])
    probs, picks, own = pl.pallas_call(
        functools.partial(_moba_score_kernel, page=page, n_pages=n_pages, group=group), grid_spec=grid_spec,
        out_shape=out_shape, compiler_params=_params("arbitrary", "arbitrary"), name="moba_score")(
        page_table, q, qrow, kn, vn, tab3, *([kt_cache] * group))

    lpage = (picks[:, :, :ntop, None] * ppb + jnp.arange(ppb)[None, None, None]).reshape(nb, N_HEADS, ntop * ppb)
    vpage = jnp.take_along_axis(page_table[:, None, :], lpage, axis=2)
    nstep = ntop * ppb
    lpage, vpage = lpage.reshape(nb, N_HEADS * nstep), vpage.reshape(nb, N_HEADS * nstep)
    v_specs = [pl.BlockSpec((None, None, HEAD_DIM, page),
                            (lambda h: lambda b, t, vp, lp: (vp[b, h * nstep + t], h, 0, 0))(h))
               for h in range(N_HEADS)]
    p_specs = [pl.BlockSpec((None, None, N_HEADS, page),
                            (lambda h: lambda b, t, vp, lp: (b, lp[b, h * nstep + t], 0, 0))(h))
               for h in range(N_HEADS)]
    col2 = pl.BlockSpec((None, N_HEADS, HEAD_DIM, 1), lambda b, t, vp, lp: (b, 0, 0, 0))
    grid_spec = pltpu.PrefetchScalarGridSpec(
        num_scalar_prefetch=2, grid=(nb, nstep), in_specs=v_specs + p_specs + [col2], out_specs=col2,
        scratch_shapes=[pltpu.VMEM((N_HEADS, HEAD_DIM, page), F32)])
    return pl.pallas_call(_moba_gather_kernel, grid_spec=grid_spec,
                          out_shape=jax.ShapeDtypeStruct((nb, N_HEADS, HEAD_DIM, 1), F32),
                          compiler_params=_params("arbitrary", "arbitrary"), name="moba_gather")(
        vpage, lpage, *([vt_cache] * N_HEADS), *([probs] * N_HEADS), own)


def _heads(a, batch, seq):
    return a.reshape(batch, seq, N_HEADS, HEAD_DIM).transpose(0, 2, 1, 3)


def kernel(x_prompt, x_sample, cache_fox_k, cache_fox_v, cache_fox_logf, cache_moba_k, cache_moba_v, page_table,
           g_attn, w_in, b_forget, g_q_fox, g_k_fox, g_q_moba, g_k_moba, t5_table, w_branch_fox, w_branch_moba,
           w_out, g_ffn, w_router_group, w_router_expert, w_e_gate, w_e_up, w_e_down):
    depth = w_in.shape[0]
    assert depth == 1, "single-layer trunk"
    batch, seq, _ = x_prompt.shape
    nb, nt, _ = x_sample.shape
    assert nt == 1 and seq % (2 * MOBA_BLOCK) == 0
    n_p, n_s = batch * seq, nb * nt

    w = w_in[0]
    f0 = 3 * W_HEADS
    w_main = jnp.concatenate([w[:, :f0], w[:, f0 + N_HEADS:]], axis=1).astype(BF16)
    w_f = jnp.pad(w[:, f0:f0 + N_HEADS], ((0, 0), (0, LANES - N_HEADS))).astype(BF16)
    b_f = jnp.pad(b_forget[0], (0, LANES - N_HEADS))[None]
    gains = jnp.stack([jnp.tile(g[0], N_HEADS) for g in (g_q_fox, g_k_fox, g_q_moba, g_k_moba)])
    gains = jnp.pad(gains, ((0, 4), (0, 0)))
    seg = (jnp.arange(W_HEADS)[:, None] // HEAD_DIM == jnp.arange(W_HEADS)[None] // HEAD_DIM)
    seg = (seg.astype(F32) / HEAD_DIM).astype(BF16)
    wbf, wbm, wout = w_branch_fox[0].astype(BF16), w_branch_moba[0].astype(BF16), w_out[0].astype(BF16)
    w_router = jnp.concatenate([w_router_group[0], w_router_expert[0].reshape(D_MODEL, N_EXPERTS)], axis=1)
    w_router = jnp.pad(w_router, ((0, 0), (0, LANES - N_GROUPS - N_EXPERTS)))
    wg, wu, wd = w_e_gate[0].astype(BF16), w_e_up[0].astype(BF16), w_e_down[0].astype(BF16)
    proj_w = (g_attn, w_main, w_f, b_f, gains, seg)

    xp = x_prompt.reshape(n_p, D_MODEL)
    (q_f, kt_f, v_f, kt_f32, vt_f32, lf, q_m, q32_m, kt_m, v_m, kt_m32, vt_m32, kmean, sga, sgb) = \
        _project_prompt(xp, batch, seq, MOBA_BLOCK, *proj_w)
    cum = _cumsum(lf, batch, seq).reshape(batch, seq, LANES)
    rows3 = lambda a: a.reshape(batch, seq, W_HEADS)
    oa = _fox_prompt(q_f, cum, kt_f, _fox_key_rows(cum[:, :, :N_HEADS]), rows3(v_f))
    member, km, own_bias, prev_delta, prev_tile = _moba_side_operands(kmean, _moba_bias(t5_table), t5_table,
                                                                      batch, seq)
    ob = _moba_prompt(t5_table, q_m, q32_m, kt_m, member, rows3(v_m), km, own_bias, prev_delta, prev_tile)
    merge_w = (wbf, wbm, wout, g_ffn, w_router)
    x1, h2, route = _merge(xp, oa.reshape(n_p, W_HEADS), ob.reshape(n_p, W_HEADS), sga, sgb, *merge_w,
                           tm=MOBA_BLOCK)
    y_prompt = _moe(x1, h2, route, wg, wu, wd, tm=min(1024, n_p)).reshape(batch, seq, D_MODEL)

    xs = x_sample.reshape(n_s, D_MODEL)
    qa_s, ka_s, va_s, lf_s, qb_s, kb_s, vb_s, sga_s, sgb_s, _ = _project(xs, n_s, *proj_w)
    tokh = lambda a: a.reshape(n_s, N_HEADS, HEAD_DIM, 1)
    rows_last = lambda c: c[0].transpose(0, 2, 3, 1)
    tokrow = lambda a: a.reshape(n_s, 1, W_HEADS)
    oa_s = _fox_decode(page_table, tokrow(qa_s), tokrow(ka_s), tokrow(va_s), lf_s[:, :N_HEADS, None],
                       rows_last(cache_fox_k), rows_last(cache_fox_v), cache_fox_logf[0].transpose(0, 2, 1))
    ob_s = _moba_decode(page_table, tokh(qb_s), tokrow(qb_s), tokh(kb_s), tokh(vb_s), t5_table[:, :, None],
                        rows_last(cache_moba_k), rows_last(cache_moba_v))
    flat = lambda o: o.reshape(n_s, W_HEADS).astype(BF16)
    x1_s, h2_s, route_s = _merge(xs, flat(oa_s), flat(ob_s), sga_s, sgb_s, *merge_w, tm=n_s)
    y_sample = _moe(x1_s, h2_s, route_s, wg, wu, wd, tm=n_s).reshape(nb, nt, D_MODEL)

    kv_p = lambda a: a.transpose(0, 3, 1, 2)[None]
    kv_s = lambda a: a.reshape(depth, nb, nt, N_HEADS, HEAD_DIM)
    return (y_prompt, y_sample,
            kv_p(kt_f32), kv_p(vt_f32), lf[:, :N_HEADS].reshape(depth, batch, seq, N_HEADS),
            kv_p(kt_m32), kv_p(vt_m32),
            kv_s(ka_s), kv_s(va_s), lf_s[:, :N_HEADS].reshape(depth, nb, nt, N_HEADS), kv_s(kb_s), kv_s(vb_s))
```

```python
import functools
import math

import numpy as np
import jax
import jax.numpy as jnp
from jax import lax
from jax.experimental import pallas as pl
from jax.experimental.pallas import tpu as pltpu

F32 = jnp.float32
BF16 = jnp.bfloat16
HIGHEST = lax.Precision.HIGHEST

D_MODEL = 1024
N_HEADS = 8
HEAD_DIM = 64
W_HEADS = N_HEADS * HEAD_DIM
MOBA_BLOCK = 256
MOBA_TOPK = 3
T5_BUCKETS = 32
T5_MAX_DIST = 128
N_GROUPS = 4
EXPERTS_PER_GROUP = 8
N_EXPERTS = N_GROUPS * EXPERTS_PER_GROUP
D_EXPERT = 256
EPS = 1e-6
NEG = -1e30
SCALE = HEAD_DIM ** -0.5
LOG2E = 1.4426950408889634
LANES = 128
ATTN_TILE = 2 * MOBA_BLOCK
VMEM_LIMIT = 56 * 1024 * 1024


def _t5_thresholds():
    max_exact = T5_BUCKETS // 2
    n = np.arange(0, 4 * T5_MAX_DIST)
    nf = np.maximum(n, max_exact).astype(np.float32)
    large = max_exact + (np.log(nf / np.float32(max_exact)) / np.float32(math.log(T5_MAX_DIST / max_exact))
                         * np.float32(T5_BUCKETS - max_exact)).astype(np.int32)
    bucket = np.where(n < max_exact, n, np.minimum(large, T5_BUCKETS - 1))
    return [int(np.argmax(bucket >= j)) for j in range(1, T5_BUCKETS)]


T5_THRESHOLDS = _t5_thresholds()


def _params(*sem):
    return pltpu.CompilerParams(dimension_semantics=sem, vmem_limit_bytes=VMEM_LIMIT)


def _proj_body(x_ref, g_ref, w_ref, wf_ref, bf_ref, gains_ref, seg_ref, precise_norm=False):
    x = x_ref[...]
    h = x * lax.rsqrt(jnp.mean(x * x, axis=-1, keepdims=True) + EPS) * g_ref[...]
    hb = h.astype(BF16)
    seg = seg_ref[...]

    def proj(i, width=W_HEADS):
        return jnp.dot(hb, w_ref[:, i:i + width], preferred_element_type=F32)

    def headnorm(z, row):
        if precise_norm:
            ms = jnp.dot(z * z, seg.astype(F32), preferred_element_type=F32, precision=HIGHEST)
        else:
            ms = jnp.dot((z * z).astype(BF16), seg, preferred_element_type=F32)
        return z * lax.rsqrt(ms + EPS) * gains_ref[row:row + 1, :]

    f = jnp.dot(hb, wf_ref[...], preferred_element_type=F32) + bf_ref[...]
    return dict(
        qa=headnorm(proj(0), 0), ka=headnorm(proj(W_HEADS), 1), va=proj(2 * W_HEADS),
        qb=headnorm(proj(3 * W_HEADS), 2), kb=headnorm(proj(4 * W_HEADS), 3), vb=proj(5 * W_HEADS),
        sga=jax.nn.sigmoid(proj(6 * W_HEADS, D_MODEL)),
        sgb=jax.nn.sigmoid(proj(6 * W_HEADS + D_MODEL, D_MODEL)),
        lf=jnp.minimum(f, 0.0) - jnp.log1p(jnp.exp(-jnp.abs(f))))


def _proj_kernel(x_ref, g_ref, w_ref, wf_ref, bf_ref, gains_ref, seg_ref,
                 qa_ref, ka_ref, va_ref, lf_ref, qb_ref, kb_ref, vb_ref, sga_ref, sgb_ref, km_ref):
    z = _proj_body(x_ref, g_ref, w_ref, wf_ref, bf_ref, gains_ref, seg_ref, precise_norm=True)
    for name, ref in (("qa", qa_ref), ("ka", ka_ref), ("va", va_ref), ("lf", lf_ref), ("qb", qb_ref),
                      ("kb", kb_ref), ("vb", vb_ref), ("sga", sga_ref), ("sgb", sgb_ref)):
        ref[...] = z[name]
    km_ref[...] = jnp.mean(z["kb"], axis=0, keepdims=True)


def _proj_prompt_kernel(x_ref, g_ref, w_ref, wf_ref, bf_ref, gains_ref, seg_ref,
                        qf_ref, ktf_ref, vf_ref, ktf32_ref, vtf32_ref, lf_ref,
                        qm_ref, qm32_ref, ktm_ref, vm_ref, ktm32_ref, vtm32_ref, km_ref, sga_ref, sgb_ref):
    z = _proj_body(x_ref, g_ref, w_ref, wf_ref, bf_ref, gains_ref, seg_ref)
    tm = x_ref.shape[0]
    lane = lax.broadcasted_iota(jnp.int32, (tm, LANES), 1)
    low = lane < HEAD_DIM
    ones = jnp.where(jnp.logical_and(lane >= HEAD_DIM, lane < HEAD_DIM + 3), 1.0, 0.0)

    def heads_low(a):
        out = []
        for p in range(N_HEADS // 2):
            pair = a[:, p * LANES:(p + 1) * LANES]
            out += [pair, pltpu.roll(pair, HEAD_DIM, 1)]
        return out

    def heads_t(a):
        return a.T.reshape(N_HEADS, HEAD_DIM, tm)

    for hh, q in enumerate(heads_low(z["qa"] * (SCALE * LOG2E))):
        qf_ref[hh] = jnp.where(low, q, ones).astype(BF16)
    for hh, q in enumerate(heads_low(z["qb"])):
        qm32_ref[hh] = q[:, :HEAD_DIM]
        qm_ref[hh] = jnp.where(low, q * (SCALE * LOG2E), 0.0).astype(BF16)
    kat, kbt = heads_t(z["ka"]), heads_t(z["kb"])
    ktf32_ref[...] = kat
    ktf_ref[...] = kat.astype(BF16)
    ktm32_ref[...] = kbt
    ktm_ref[...] = kbt.astype(BF16)
    vtf32_ref[...] = heads_t(z["va"])
    vtm32_ref[...] = heads_t(z["vb"])
    vf_ref[...] = z["va"].astype(BF16)
    vm_ref[...] = z["vb"].astype(BF16)
    km_ref[...] = jnp.mean(z["kb"], axis=0, keepdims=True)
    lf_ref[...] = z["lf"]
    sga_ref[...] = z["sga"].astype(BF16)
    sgb_ref[...] = z["sgb"].astype(BF16)


def _project_prompt(x2d, batch, seq, tm, g_attn, w_main, w_f, b_f, gains, seg):
    n = x2d.shape[0]
    tpb = seq // tm
    row = lambda i: (i, 0)
    const = lambda i: (0, 0)
    blk = lambda w: pl.BlockSpec((tm, w), row)
    qspec = lambda w: pl.BlockSpec((None, N_HEADS, tm, w), lambda i: (i // tpb, 0, i % tpb, 0))
    tspec = pl.BlockSpec((None, N_HEADS, HEAD_DIM, tm), lambda i: (i // tpb, 0, 0, i % tpb))
    hm = lambda w, dt: jax.ShapeDtypeStruct((batch, N_HEADS, seq, w), dt)
    tr = lambda dt: jax.ShapeDtypeStruct((batch, N_HEADS, HEAD_DIM, seq), dt)
    wide = lambda w, dt: jax.ShapeDtypeStruct((n, w), dt)
    out_shape = (hm(LANES, BF16), tr(BF16), wide(W_HEADS, BF16), tr(F32), tr(F32), wide(LANES, F32),
                 hm(LANES, BF16), hm(HEAD_DIM, F32), tr(BF16), wide(W_HEADS, BF16), tr(F32), tr(F32),
                 jax.ShapeDtypeStruct((n // tm, 1, W_HEADS), F32), wide(D_MODEL, BF16), wide(D_MODEL, BF16))
    out_specs = (qspec(LANES), tspec, blk(W_HEADS), tspec, tspec, blk(LANES),
                 qspec(LANES), qspec(HEAD_DIM), tspec, blk(W_HEADS), tspec, tspec,
                 pl.BlockSpec((None, 1, W_HEADS), lambda i: (i, 0, 0)), blk(D_MODEL), blk(D_MODEL))
    in_specs = [blk(D_MODEL), pl.BlockSpec((1, D_MODEL), const), pl.BlockSpec(w_main.shape, const),
                pl.BlockSpec(w_f.shape, const), pl.BlockSpec((1, LANES), const),
                pl.BlockSpec(gains.shape, const), pl.BlockSpec(seg.shape, const)]
    return pl.pallas_call(_proj_prompt_kernel, grid=(n // tm,), in_specs=in_specs, out_specs=out_specs,
                          out_shape=out_shape, compiler_params=_params("arbitrary"), name="proj_prompt")(
        x2d, g_attn, w_main, w_f, b_f, gains, seg)


def _project(x2d, tm, g_attn, w_main, w_f, b_f, gains, seg):
    n = x2d.shape[0]
    row = lambda i: (i, 0)
    const = lambda i: (0, 0)
    wide = lambda w, dt: jax.ShapeDtypeStruct((n, w), dt)
    out_shape = (wide(W_HEADS, F32), wide(W_HEADS, F32), wide(W_HEADS, F32), wide(LANES, F32),
                 wide(W_HEADS, F32), wide(W_HEADS, F32), wide(W_HEADS, F32),
                 wide(D_MODEL, F32), wide(D_MODEL, F32),
                 jax.ShapeDtypeStruct((n // tm, 1, W_HEADS), F32))
    blk = lambda w: pl.BlockSpec((tm, w), row)
    out_specs = (blk(W_HEADS), blk(W_HEADS), blk(W_HEADS), blk(LANES), blk(W_HEADS), blk(W_HEADS), blk(W_HEADS),
                 blk(D_MODEL), blk(D_MODEL), pl.BlockSpec((None, 1, W_HEADS), lambda i: (i, 0, 0)))
    in_specs = [blk(D_MODEL), pl.BlockSpec((1, D_MODEL), const), pl.BlockSpec(w_main.shape, const),
                pl.BlockSpec(w_f.shape, const), pl.BlockSpec((1, LANES), const),
                pl.BlockSpec(gains.shape, const), pl.BlockSpec(seg.shape, const)]
    return pl.pallas_call(_proj_kernel, grid=(n // tm,), in_specs=in_specs, out_specs=out_specs,
                          out_shape=out_shape, compiler_params=_params("arbitrary"), name="proj")(
        x2d, g_attn, w_main, w_f, b_f, gains, seg)


def _cumsum_kernel(lf_ref, o_ref, carry_ref, *, tc):
    @pl.when(pl.program_id(1) == 0)
    def _():
        carry_ref[...] = jnp.zeros_like(carry_ref)

    r = lax.broadcasted_iota(jnp.int32, (tc, tc), 0)
    c = lax.broadcasted_iota(jnp.int32, (tc, tc), 1)
    tri = jnp.where(c <= r, 1.0, 0.0).astype(F32)
    out = jnp.dot(tri, lf_ref[...], preferred_element_type=F32, precision=HIGHEST) + carry_ref[...]
    o_ref[...] = out
    carry_ref[...] = out[tc - 1:tc, :]


def _cumsum(lf, batch, seq, tc=256):
    nchunk = seq // tc
    spec = pl.BlockSpec((tc, LANES), lambda b, i: (b * nchunk + i, 0))
    return pl.pallas_call(functools.partial(_cumsum_kernel, tc=tc), grid=(batch, nchunk),
                          in_specs=[spec], out_specs=spec,
                          out_shape=jax.ShapeDtypeStruct(lf.shape, F32),
                          scratch_shapes=[pltpu.VMEM((1, LANES), F32)],
                          compiler_params=_params("arbitrary", "arbitrary"), name="logf_cumsum")(lf)


def _online_update(carry, s2, v):
    m, l, acc = carry
    m_new = jnp.maximum(m, jnp.max(s2, axis=-1, keepdims=True))
    alpha = jnp.exp2(m - m_new)
    p = jnp.exp2(s2 - m_new)
    l = alpha * l + jnp.sum(p, axis=-1, keepdims=True)
    acc = alpha * acc + jnp.dot(p.astype(BF16), v, preferred_element_type=F32)
    return m_new, l, acc


def _softmax_init(rows, width=LANES):
    return (jnp.full((rows, 1), NEG, F32), jnp.zeros((rows, 1), F32), jnp.zeros((rows, width), F32))


def _pair_output(results):
    (_, l0, a0), (_, l1, a1) = results
    lane = lax.broadcasted_iota(jnp.int32, a0.shape, 1)
    return jnp.where(lane < HEAD_DIM, a0 / l0, a1 / l1)


def _split3(c):
    def top(x):
        bits = lax.bitcast_convert_type(x, jnp.uint32) & jnp.uint32(0xFFFF0000)
        return lax.bitcast_convert_type(bits, F32)
    hi = top(c)
    mid = top(c - hi)
    return hi, mid, c - hi - mid


FOX_EXTRA_ROWS = 16


def _fox_kernel(q_ref, cum_ref, kt_ref, kx_ref, v_ref, o_ref, *, tq):
    pair, qi = pl.program_id(1), pl.program_id(2)
    lane = lax.broadcasted_iota(jnp.int32, (tq, LANES), 1)
    row = lax.broadcasted_iota(jnp.int32, (tq, tq), 0)
    col = lax.broadcasted_iota(jnp.int32, (tq, tq), 1)
    cum = cum_ref[...] * LOG2E
    results = []
    for hh in range(2):
        cq = jnp.sum(jnp.where(lane == 2 * pair + hh, cum, 0.0), axis=-1, keepdims=True)
        q = q_ref[hh].astype(F32)
        for piece, value in enumerate(_split3(cq)):
            q = jnp.where(lane == HEAD_DIM + 3 + piece, value, q)
        q = q.astype(BF16)

        def tile(k0, width=tq, q=q, hh=hh):
            k0 = pl.multiple_of(k0, tq)
            kt = jnp.concatenate([kt_ref[hh, :, pl.ds(k0, width)], kx_ref[hh, :, pl.ds(k0, width)],
                                  jnp.zeros((LANES - HEAD_DIM - FOX_EXTRA_ROWS, width), BF16)], axis=0)
            return jnp.dot(q, kt, preferred_element_type=F32), v_ref[pl.ds(k0, width), :]

        carry = lax.fori_loop(0, qi // 2, lambda c, carry, tile=tile: _online_update(carry, *tile(c * 2 * tq, 2 * tq)),
                              _softmax_init(tq))
        carry = lax.fori_loop(qi - qi % 2, qi, lambda c, carry, tile=tile: _online_update(carry, *tile(c * tq)), carry)
        s2, v = tile(qi * tq)
        results.append(_online_update(carry, jnp.where(col <= row, s2, NEG), v))
    o_ref[...] = _pair_output(results).astype(o_ref.dtype)


def _fox_prompt(q_aug, cum, kt, kx, v, tq=ATTN_TILE):
    b, h, s, _ = q_aug.shape
    pair = lambda shape, imap: pl.BlockSpec((None, 2) + shape, imap)
    in_specs = [pair((tq, LANES), lambda b, p, i: (b, p, i, 0)),
                pl.BlockSpec((None, tq, LANES), lambda b, p, i: (b, i, 0)),
                pair((HEAD_DIM, s), lambda b, p, i: (b, p, 0, 0)),
                pair((FOX_EXTRA_ROWS, s), lambda b, p, i: (b, p, 0, 0)),
                pl.BlockSpec((None, s, LANES), lambda b, p, i: (b, 0, p))]
    out_specs = pl.BlockSpec((None, tq, LANES), lambda b, p, i: (b, i, p))
    return pl.pallas_call(functools.partial(_fox_kernel, tq=tq), grid=(b, h // 2, s // tq), in_specs=in_specs,
                          out_specs=out_specs, out_shape=jax.ShapeDtypeStruct((b, s, h * HEAD_DIM), BF16),
                          compiler_params=_params("arbitrary", "arbitrary", "arbitrary"), name="fox_prompt")(
        q_aug, cum, kt, kx, v)


def _fox_key_rows(cum):
    pieces = _split3((cum * LOG2E).transpose(0, 2, 1))
    ones = jnp.ones_like(pieces[0])
    zero = jnp.zeros_like(pieces[0])
    return jnp.stack([-p for p in pieces] + [ones] * 3 + [zero] * (FOX_EXTRA_ROWS - 6), axis=2).astype(BF16)


def _t5_select(dist, table_at):
    bias = table_at(0)
    for j, thr in enumerate(T5_THRESHOLDS, start=1):
        bias = jnp.where(dist >= thr, table_at(j), bias)
    return bias


def _bias_kernel(tab_ref, o_ref):
    hh = pl.program_id(0)
    i = lax.broadcasted_iota(jnp.int32, (MOBA_BLOCK, 2 * MOBA_BLOCK), 0)
    j = lax.broadcasted_iota(jnp.int32, (MOBA_BLOCK, 2 * MOBA_BLOCK), 1)
    dist = MOBA_BLOCK + i - j
    o_ref[...] = _t5_select(dist, lambda bkt: tab_ref[bkt, hh])


def _moba_bias(t5_table):
    return pl.pallas_call(_bias_kernel, grid=(N_HEADS,),
                          in_specs=[pl.BlockSpec(memory_space=pltpu.SMEM)],
                          out_specs=pl.BlockSpec((None, MOBA_BLOCK, 2 * MOBA_BLOCK), lambda h: (h, 0, 0)),
                          out_shape=jax.ShapeDtypeStruct((N_HEADS, MOBA_BLOCK, 2 * MOBA_BLOCK), F32),
                          compiler_params=_params("arbitrary"), name="moba_bias")(t5_table)


MOBA_MAX_BLOCKS = 32
SEL_LANE0 = HEAD_DIM
SEL_LANE1 = HEAD_DIM + MOBA_MAX_BLOCKS


def _moba_kernel(tab_ref, qp_ref, q32_ref, kt_ref, member_ref, v_ref, km_ref, own_ref, prev_ref, prevtile_ref,
                 o_ref, qsel_ref):
    blk, tq = MOBA_BLOCK, 2 * MOBA_BLOCK
    pair, i = pl.program_id(1), pl.program_id(2)
    nblk = MOBA_MAX_BLOCKS
    dot = functools.partial(jnp.dot, preferred_element_type=F32)
    blk_i = lax.broadcasted_iota(jnp.int32, (nblk, tq), 0)
    blk_f = blk_i.astype(F32)
    own = 2 * i + (lax.broadcasted_iota(jnp.int32, (nblk, tq), 1) >= blk).astype(jnp.int32)
    lane = lax.broadcasted_iota(jnp.int32, (tq, LANES), 1)
    causal = lax.broadcasted_iota(jnp.int32, (blk, blk), 1) <= lax.broadcasted_iota(jnp.int32, (blk, blk), 0)
    k0 = pl.multiple_of(i * tq, tq)
    v_d = v_ref[pl.ds(k0, tq), :]
    results = []
    for hh in range(2):
        qp = qp_ref[hh]
        gate = lax.dot_general(km_ref[hh].astype(BF16), q32_ref[hh].astype(BF16), (((1,), (1,)), ((), ())),
                               preferred_element_type=F32)
        g = jnp.where(blk_i < own, gate, -jnp.inf)
        chosen = jnp.zeros((nblk, tq), F32)
        for _ in range(MOBA_TOPK):
            mx = jnp.max(g, axis=0, keepdims=True)
            idx = jnp.min(jnp.where(g == mx, blk_f, float(nblk)), axis=0, keepdims=True)
            pick = jnp.logical_and(blk_f == idx, mx > -jnp.inf)
            chosen = jnp.where(pick, 1.0, chosen)
            g = jnp.where(pick, -jnp.inf, g)
        far = jnp.full((1, 1), tab_ref[T5_BUCKETS - 1, 2 * pair + hh] * LOG2E, F32)
        far_hi = far.astype(BF16).astype(F32)
        picked = chosen > 0.5
        sel_t = jnp.concatenate([jnp.zeros((SEL_LANE0, tq), F32), jnp.where(picked, far_hi, NEG),
                                 jnp.where(picked, far - far_hi, 0.0)], axis=0)
        qsel_ref[hh] = jnp.where(lane >= SEL_LANE0, sel_t.T.astype(BF16), qp)

        def keys(k0, width=tq, hh=hh):
            return jnp.concatenate([kt_ref[hh, :, pl.ds(k0, width)], member_ref[:, pl.ds(k0, width)]], axis=0)

        kt_d = keys(k0)
        own_bias = own_ref[hh]
        s_a = jnp.where(causal, dot(qp[:blk], kt_d[:, :blk]) + own_bias, NEG)
        s_b = dot(qsel_ref[hh, blk:, :], kt_d[:, :blk]) + prev_ref[hh]
        s_c = jnp.where(causal, dot(qp[blk:], kt_d[:, blk:]) + own_bias, NEG)
        s_d = jnp.concatenate([jnp.concatenate([s_a, jnp.full((blk, blk), NEG, F32)], axis=1),
                               jnp.concatenate([s_b, s_c], axis=1)], axis=0)
        carry = _online_update(_softmax_init(tq), s_d, v_d)

        def tile(c, carry, near=False, tiles=1, hh=hh, keys=keys):
            k0 = pl.multiple_of(c * tiles * tq, tq)
            s2 = dot(qsel_ref[hh], keys(k0, tiles * tq))
            if near:
                s2 = s2 + prevtile_ref[hh]
            return _online_update(carry, s2, v_ref[pl.ds(k0, tiles * tq), :])

        n_far = jnp.maximum(i - 1, 0)
        carry = lax.fori_loop(n_far, i, functools.partial(tile, near=True), carry)
        carry = lax.fori_loop(0, n_far // 2, functools.partial(tile, tiles=2), carry)
        results.append(lax.fori_loop(n_far - n_far % 2, n_far, tile, carry))
    o_ref[...] = _pair_output(results).astype(o_ref.dtype)


def _moba_prompt(t5_table, q_pad, q32, kt, member, v, km, own_bias, prev_delta, prev_tile):
    b, h, s, _ = q_pad.shape
    blk, tq = MOBA_BLOCK, 2 * MOBA_BLOCK
    pair = lambda shape, imap: pl.BlockSpec((None, 2) + shape, imap)
    per_pair = lambda shape: pl.BlockSpec((2,) + shape, lambda b, p, i: (p, 0, 0))
    in_specs = [pl.BlockSpec(memory_space=pltpu.SMEM),
                pair((tq, LANES), lambda b, p, i: (b, p, i, 0)),
                pair((tq, HEAD_DIM), lambda b, p, i: (b, p, i, 0)),
                pair((HEAD_DIM, s), lambda b, p, i: (b, p, 0, 0)),
                pl.BlockSpec((2 * MOBA_MAX_BLOCKS, s), lambda b, p, i: (0, 0)),
                pl.BlockSpec((None, s, LANES), lambda b, p, i: (b, 0, p)),
                pair((MOBA_MAX_BLOCKS, HEAD_DIM), lambda b, p, i: (b, p, 0, 0)),
                per_pair((blk, blk)), per_pair((blk, blk)), per_pair((tq, tq))]
    out_specs = pl.BlockSpec((None, tq, LANES), lambda b, p, i: (b, i, p))
    return pl.pallas_call(_moba_kernel, grid=(b, h // 2, s // tq), in_specs=in_specs, out_specs=out_specs,
                          out_shape=jax.ShapeDtypeStruct((b, s, h * HEAD_DIM), BF16),
                          scratch_shapes=[pltpu.VMEM((2, tq, LANES), BF16)],
                          compiler_params=_params("arbitrary", "arbitrary", "arbitrary"), name="moba_prompt")(
        t5_table, q_pad, q32, kt, member, v, km, own_bias, prev_delta, prev_tile)


def _moba_side_operands(kmean, bias_tiles, t5_table, batch, seq):
    blk = MOBA_BLOCK
    nb = seq // blk
    assert nb <= MOBA_MAX_BLOCKS
    member = (jnp.arange(MOBA_MAX_BLOCKS)[:, None] == jnp.arange(seq)[None] // blk).astype(BF16)
    member = jnp.concatenate([member, member], axis=0)
    km = kmean.reshape(batch, nb, N_HEADS, HEAD_DIM).transpose(0, 2, 1, 3)
    km = jnp.pad(km, ((0, 0), (0, 0), (0, MOBA_MAX_BLOCKS - nb), (0, 0)))
    far = t5_table[T5_BUCKETS - 1][:, None, None]
    own_bias = bias_tiles[:, :, blk:] * LOG2E
    prev_delta = (bias_tiles[:, :, :blk] - far) * LOG2E
    prev_tile = jnp.zeros((N_HEADS, 2 * blk, 2 * blk), F32).at[:, :blk, blk:].set(prev_delta)
    return member, km, own_bias, prev_delta, prev_tile


def _merge_kernel(x_ref, oa_ref, ob_ref, sga_ref, sgb_ref, wbf_ref, wbm_ref, wout_ref, g_ref, wr_ref,
                  x1_ref, h2_ref, route_ref):
    ma = jnp.dot(oa_ref[...], wbf_ref[...], preferred_element_type=F32)
    mb = jnp.dot(ob_ref[...], wbm_ref[...], preferred_element_type=F32)
    mix = sga_ref[...].astype(F32) * ma + sgb_ref[...].astype(F32) * mb
    x1 = x_ref[...] + jnp.dot(mix.astype(BF16), wout_ref[...], preferred_element_type=F32)
    x1_ref[...] = x1
    h2 = x1 * lax.rsqrt(jnp.mean(x1 * x1, axis=-1, keepdims=True) + EPS) * g_ref[...]
    h2_ref[...] = h2.astype(BF16)

    logits = jnp.dot(h2.astype(BF16), wr_ref[...].astype(BF16), preferred_element_type=F32)
    lane = lax.broadcasted_iota(jnp.int32, logits.shape, 1)
    big = LANES

    def argmax_low(vals):
        mx = jnp.max(vals, axis=-1, keepdims=True)
        return mx, jnp.min(jnp.where(vals == mx, lane, big), axis=-1, keepdims=True)

    glog = jnp.where(lane < N_GROUPS, logits, -jnp.inf)
    gmax, gsel = argmax_low(glog)
    gprob = 1.0 / jnp.sum(jnp.exp(glog - gmax), axis=-1, keepdims=True)
    first = N_GROUPS + gsel * EXPERTS_PER_GROUP
    in_group = jnp.logical_and(lane >= first, lane < first + EXPERTS_PER_GROUP)
    elog = jnp.where(in_group, logits, -jnp.inf)
    v0, i0 = argmax_low(elog)
    v1, i1 = argmax_low(jnp.where(lane == i0, -jnp.inf, elog))
    e1 = jnp.exp(v1 - v0)
    w0 = gprob / (1.0 + e1)
    w1 = gprob * e1 / (1.0 + e1)
    route = jnp.where(lane == 0, (i0 - N_GROUPS).astype(F32),
                      jnp.where(lane == 1, (i1 - N_GROUPS).astype(F32),
                                jnp.where(lane == 2, w0, jnp.where(lane == 3, w1, 0.0))))
    route_ref[...] = route


def _merge(x2d, oa, ob, sga, sgb, wbf, wbm, wout, g_ffn, w_router, tm):
    n = x2d.shape[0]
    row = lambda i: (i, 0)
    const = lambda i: (0, 0)
    blk = lambda w: pl.BlockSpec((tm, w), row)
    full = lambda a: pl.BlockSpec(a.shape, const)
    in_specs = [blk(D_MODEL), blk(W_HEADS), blk(W_HEADS), blk(D_MODEL), blk(D_MODEL),
                full(wbf), full(wbm), full(wout), full(g_ffn), full(w_router)]
    out_shape = (jax.ShapeDtypeStruct((n, D_MODEL), F32), jax.ShapeDtypeStruct((n, D_MODEL), BF16),
                 jax.ShapeDtypeStruct((n, LANES), F32))
    return pl.pallas_call(_merge_kernel, grid=(n // tm,), in_specs=in_specs,
                          out_specs=(blk(D_MODEL), blk(D_MODEL), blk(LANES)), out_shape=out_shape,
                          compiler_params=_params("arbitrary"), name="merge_router")(
        x2d, oa, ob, sga, sgb, wbf, wbm, wout, g_ffn, w_router)


def _moe_kernel(x1_ref, h2_ref, route_ref, wg_ref, wu_ref, wd_ref, o_ref):
    e = pl.program_id(1)

    @pl.when(e == 0)
    def _():
        o_ref[...] = x1_ref[...]

    h = h2_ref[...]
    a = jax.nn.silu(jnp.dot(h, wg_ref[...], preferred_element_type=F32)) * \
        jnp.dot(h, wu_ref[...], preferred_element_type=F32)
    y = jnp.dot(a.astype(BF16), wd_ref[...], preferred_element_type=F32)
    r = route_ref[...]
    ef = e.astype(F32)
    gate = jnp.where(r[:, 0:1] == ef, r[:, 2:3], 0.0) + jnp.where(r[:, 1:2] == ef, r[:, 3:4], 0.0)
    o_ref[...] += y * gate


def _moe(x1, h2, route, wg, wu, wd, tm):
    n = x1.shape[0]
    row = lambda i, e: (i, 0)
    in_specs = [pl.BlockSpec((tm, D_MODEL), row), pl.BlockSpec((tm, D_MODEL), row), pl.BlockSpec((tm, LANES), row),
                pl.BlockSpec((None, D_MODEL, D_EXPERT), lambda i, e: (e, 0, 0)),
                pl.BlockSpec((None, D_MODEL, D_EXPERT), lambda i, e: (e, 0, 0)),
                pl.BlockSpec((None, D_EXPERT, D_MODEL), lambda i, e: (e, 0, 0))]
    return pl.pallas_call(_moe_kernel, grid=(n // tm, N_EXPERTS), in_specs=in_specs,
                          out_specs=pl.BlockSpec((tm, D_MODEL), row),
                          out_shape=jax.ShapeDtypeStruct((n, D_MODEL), F32),
                          compiler_params=_params("arbitrary", "arbitrary"), name="experts")(
        x1, h2, route, wg, wu, wd)


DECODE_PAGES_PER_STEP = 32


def _as_mxu(x):
    return x.astype(BF16).astype(F32)


def _head_blocks(rows):
    hh = lax.broadcasted_iota(jnp.int32, (rows, W_HEADS), 0)
    ll = lax.broadcasted_iota(jnp.int32, (rows, W_HEADS), 1)
    return ll // HEAD_DIM == hh


def _fox_decode_kernel(pt_ref, q_ref, kn_ref, vn_ref, lfn_ref, *rest, page, group):
    k_refs, v_refs, lf_refs = rest[:group], rest[group:2 * group], rest[2 * group:3 * group]
    o_ref, s_ref, m_ref, l_ref, acc_ref, carry_ref = rest[3 * group:]
    phase, j = pl.program_id(1), pl.program_id(2)
    blocks = _head_blocks(N_HEADS)
    q_rows = jnp.where(blocks, _as_mxu(q_ref[...]), 0.0)
    s_new = jnp.sum(q_rows * _as_mxu(kn_ref[...]), axis=-1, keepdims=True) * SCALE

    @pl.when(phase == 0)
    def _():
        @pl.when(j == 0)
        def _():
            m_ref[...] = s_new
            carry_ref[...] = lfn_ref[...]

        r0 = lax.broadcasted_iota(jnp.int32, (page, page), 0)
        r1 = lax.broadcasted_iota(jnp.int32, (page, page), 1)
        later = jnp.where(r0 > r1, 1.0, 0.0).astype(F32)
        q_b = q_rows.astype(BF16)
        carry = carry_ref[...]
        m = m_ref[...]
        for g in range(group):
            lf = lf_refs[g][...]
            decay = jnp.dot(lf, later, preferred_element_type=F32, precision=HIGHEST) + carry
            keys = k_refs[g][...].reshape(W_HEADS, page).astype(BF16)
            s = jnp.dot(q_b, keys, preferred_element_type=F32) * SCALE + decay
            s_ref[pl.ds(j * group + g, 1)] = s[None]
            m = jnp.maximum(m, jnp.max(s, axis=-1, keepdims=True))
            carry = carry + jnp.sum(lf, axis=-1, keepdims=True)
        carry_ref[...] = carry
        m_ref[...] = m

    @pl.when(phase == 1)
    def _():
        m = m_ref[...]

        @pl.when(j == 0)
        def _():
            e = jnp.exp(s_ref[...] - m[None])
            l = jnp.sum(jnp.sum(e, axis=0), axis=-1, keepdims=True) + jnp.exp(s_new - m)
            l_ref[...] = l
            acc_ref[...] = _as_mxu(jnp.exp(s_new - m) / l) * _as_mxu(vn_ref[...])

        l = l_ref[...]
        acc = acc_ref[...]
        for g in range(group):
            p = (jnp.exp(s_ref[j * group + g] - m) / l).astype(BF16)
            values = v_refs[g][...].reshape(W_HEADS, page).astype(BF16)
            acc = acc + lax.dot_general(p, values, (((1,), (1,)), ((), ())), preferred_element_type=F32)
        acc_ref[...] = acc

        @pl.when(j == pl.num_programs(2) - 1)
        def _():
            o_ref[...] = jnp.sum(jnp.where(blocks, acc_ref[...], 0.0), axis=0, keepdims=True)


def _fox_decode(page_table, q, kn, vn, lfn, kt_cache, vt_cache, lft_cache):
    nb, n_pages = page_table.shape
    page = kt_cache.shape[-1]
    group = math.gcd(DECODE_PAGES_PER_STEP, n_pages)
    steps = n_pages // group
    tok3 = lambda b, ph, j, pt: (b, 0, 0)

    def paged(g, ndim, key_side):
        def index(b, ph, j, pt):
            step = j * (1 - ph) + (steps - 1) * ph if key_side else j * ph
            return (pt[b, n_pages - 1 - (step * group + g)],) + (0,) * ndim
        return index

    row = pl.BlockSpec((None, 1, W_HEADS), tok3)
    k_specs = [pl.BlockSpec((None, N_HEADS, HEAD_DIM, page), paged(g, 3, True)) for g in range(group)]
    v_specs = [pl.BlockSpec((None, N_HEADS, HEAD_DIM, page), paged(g, 3, False)) for g in range(group)]
    lf_specs = [pl.BlockSpec((None, N_HEADS, page), paged(g, 2, True)) for g in range(group)]
    in_specs = [row, row, row, pl.BlockSpec((None, N_HEADS, 1), tok3)] + k_specs + v_specs + lf_specs
    grid_spec = pltpu.PrefetchScalarGridSpec(
        num_scalar_prefetch=1, grid=(nb, 2, steps), in_specs=in_specs, out_specs=row,
        scratch_shapes=[pltpu.VMEM((n_pages, N_HEADS, page), F32),
                        pltpu.VMEM((N_HEADS, 1), F32), pltpu.VMEM((N_HEADS, 1), F32),
                        pltpu.VMEM((N_HEADS, W_HEADS), F32), pltpu.VMEM((N_HEADS, 1), F32)])
    return pl.pallas_call(functools.partial(_fox_decode_kernel, page=page, group=group), grid_spec=grid_spec,
                          out_shape=jax.ShapeDtypeStruct((nb, 1, W_HEADS), F32),
                          compiler_params=_params("arbitrary", "arbitrary", "arbitrary"), name="fox_decode")(
        page_table, q, kn, vn, lfn, *([kt_cache] * group), *([vt_cache] * group), *([lft_cache] * group))


def _moba_score_kernel(pt_ref, q_ref, qrow_ref, kn_ref, vn_ref, tab_ref, *rest, page, n_pages, group):
    k_refs = rest[:group]
    p_ref, sel_ref, own_ref, s_ref, gate_ref = rest[group:]
    j = pl.program_id(1)
    ppb = MOBA_BLOCK // page
    nblk = n_pages // ppb
    p_len = n_pages * page
    q3 = _as_mxu(q_ref[...])
    q_b = jnp.where(_head_blocks(N_HEADS), qrow_ref[...], 0.0).astype(BF16)
    for g in range(group):
        keys = k_refs[g][...].reshape(W_HEADS, page).astype(BF16)
        s_ref[pl.ds(j * group + g, 1)] = jnp.dot(q_b, keys, preferred_element_type=F32)[None]
    for g0 in range(0, group, ppb):
        kblk = k_refs[g0][...]
        for t in range(1, ppb):
            kblk = kblk + k_refs[g0 + t][...]
        kmean = jnp.sum(kblk, axis=-1, keepdims=True) * (1.0 / MOBA_BLOCK)
        gate_ref[pl.ds((j * group + g0) // ppb, 1)] = jnp.sum(q3 * _as_mxu(kmean), axis=1)[None]

    @pl.when(j == pl.num_programs(1) - 1)
    def _():
        s = s_ref[...].reshape(nblk, ppb, N_HEADS, page)
        gate = gate_ref[...]
        idx = lax.broadcasted_iota(jnp.int32, gate.shape, 0)
        lane = lax.broadcasted_iota(jnp.int32, (N_HEADS, LANES), 1)
        chosen = jnp.zeros(gate.shape, F32)
        picks = jnp.zeros((N_HEADS, LANES), jnp.int32)
        for r in range(min(MOBA_TOPK, nblk)):
            mx = jnp.max(gate, axis=0, keepdims=True)
            first = jnp.min(jnp.where(gate == mx, idx, nblk), axis=0, keepdims=True)
            pick = idx == first
            chosen = jnp.where(pick, 1.0, chosen)
            gate = jnp.where(pick, -jnp.inf, gate)
            picks = jnp.where(lane == r, first[0], picks)
        sel_ref[...] = picks

        logits = s * SCALE + tab_ref[T5_BUCKETS - 1][None, None]
        masked = jnp.where(chosen[:, None] > 0.5, logits, NEG).reshape(n_pages, N_HEADS, page)
        s_ref[...] = masked
        for pg in range(n_pages):
            if p_len - (pg * page + page - 1) < T5_THRESHOLDS[-1]:
                dist = p_len - pg * page - lax.broadcasted_iota(jnp.int32, (1, page), 1)
                bias = _t5_select(dist, lambda bkt: tab_ref[bkt])
                near = s[pg // ppb, pg % ppb] * SCALE + bias
                s_ref[pg] = jnp.where(chosen[pg // ppb] > 0.5, near, NEG)
        masked = s_ref[...]
        s_own = jnp.sum(q3 * _as_mxu(kn_ref[...]), axis=1) * SCALE + tab_ref[0]
        m = jnp.maximum(jnp.max(jnp.max(masked, axis=0), axis=-1, keepdims=True), s_own)
        p = jnp.exp(masked - m[None])
        p_own = jnp.exp(s_own - m)
        l = jnp.sum(jnp.sum(p, axis=0), axis=-1, keepdims=True) + p_own
        p_ref[...] = _as_mxu(p / l[None])
        own_ref[...] = _as_mxu(p_own / l)[:, :, None] * _as_mxu(vn_ref[...])


def _moba_gather_kernel(vpage_ref, lpage_ref, *rest):
    v_refs, p_refs = rest[:N_HEADS], rest[N_HEADS:2 * N_HEADS]
    own_ref, o_ref, acc_ref = rest[2 * N_HEADS:]
    t = pl.program_id(1)

    @pl.when(t == 0)
    def _():
        acc_ref[...] = jnp.zeros_like(acc_ref)

    for h in range(N_HEADS):
        acc_ref[h] += _as_mxu(v_refs[h][...]) * p_refs[h][h:h + 1, :]

    @pl.when(t == pl.num_programs(1) - 1)
    def _():
        o_ref[...] = jnp.sum(acc_ref[...], axis=-1, keepdims=True) + own_ref[...]


def _moba_decode(page_table, q, qrow, kn, vn, tab3, kt_cache, vt_cache):
    nb, n_pages = page_table.shape
    page = kt_cache.shape[-1]
    ppb = MOBA_BLOCK // page
    nblk = n_pages // ppb
    ntop = min(MOBA_TOPK, nblk)
    group = math.gcd(DECODE_PAGES_PER_STEP, n_pages)
    assert group % ppb == 0, "a grid step must hold whole MoBA blocks"
    tok4 = lambda b, j, pt: (b, 0, 0, 0)
    col = pl.BlockSpec((None, N_HEADS, HEAD_DIM, 1), tok4)
    k_specs = [pl.BlockSpec((None, N_HEADS, HEAD_DIM, page),
                            (lambda g: lambda b, j, pt: (pt[b, j * group + g], 0, 0, 0))(g)) for g in range(group)]
    in_specs = [col, pl.BlockSpec((None, 1, W_HEADS), lambda b, j, pt: (b, 0, 0)), col, col,
                pl.BlockSpec((T5_BUCKETS, N_HEADS, 1), lambda b, j, pt: (0, 0, 0))] + k_specs
    out_specs = (pl.BlockSpec((None, n_pages, N_HEADS, page), tok4),
                 pl.BlockSpec((None, N_HEADS, LANES), lambda b, j, pt: (b, 0, 0)), col)
    out_shape = (jax.ShapeDtypeStruct((nb, n_pages, N_HEADS, page), F32),
                 jax.ShapeDtypeStruct((nb, N_HEADS, LANES), jnp.int32),
                 jax.ShapeDtypeStruct((nb, N_HEADS, HEAD_DIM, 1), F32))
    grid_spec = pltpu.PrefetchScalarGridSpec(
        num_scalar_prefetch=1, grid=(nb, n_pages // group), in_specs=in_specs, out_specs=out_specs,
        scratch_shapes=[pltpu.VMEM((n_pages, N_HEADS, page), F32), pltpu.VMEM((nblk, N_HEADS, 1), F32)])
    probs, picks, own = pl.pallas_call(
        functools.partial(_moba_score_kernel, page=page, n_pages=n_pages, group=group), grid_spec=grid_spec,
        out_shape=out_shape, compiler_params=_params("arbitrary", "arbitrary"), name="moba_score")(
        page_table, q, qrow, kn, vn, tab3, *([kt_cache] * group))

    lpage = (picks[:, :, :ntop, None] * ppb + jnp.arange(ppb)[None, None, None]).reshape(nb, N_HEADS, ntop * ppb)
    vpage = jnp.take_along_axis(page_table[:, None, :], lpage, axis=2)
    nstep = ntop * ppb
    lpage, vpage = lpage.reshape(nb, N_HEADS * nstep), vpage.reshape(nb, N_HEADS * nstep)
    v_specs = [pl.BlockSpec((None, None, HEAD_DIM, page),
                            (lambda h: lambda b, t, vp, lp: (vp[b, h * nstep + t], h, 0, 0))(h))
               for h in range(N_HEADS)]
    p_specs = [pl.BlockSpec((None, None, N_HEADS, page),
                            (lambda h: lambda b, t, vp, lp: (b, lp[b, h * nstep + t], 0, 0))(h))
               for h in range(N_HEADS)]
    col2 = pl.BlockSpec((None, N_HEADS, HEAD_DIM, 1), lambda b, t, vp, lp: (b, 0, 0, 0))
    grid_spec = pltpu.PrefetchScalarGridSpec(
        num_scalar_prefetch=2, grid=(nb, nstep), in_specs=v_specs + p_specs + [col2], out_specs=col2,
        scratch_shapes=[pltpu.VMEM((N_HEADS, HEAD_DIM, page), F32)])
    return pl.pallas_call(_moba_gather_kernel, grid_spec=grid_spec,
                          out_shape=jax.ShapeDtypeStruct((nb, N_HEADS, HEAD_DIM, 1), F32),
                          compiler_params=_params("arbitrary", "arbitrary"), name="moba_gather")(
        vpage, lpage, *([vt_cache] * N_HEADS), *([probs] * N_HEADS), own)


def _heads(a, batch, seq):
    return a.reshape(batch, seq, N_HEADS, HEAD_DIM).transpose(0, 2, 1, 3)


def kernel(x_prompt, x_sample, cache_fox_k, cache_fox_v, cache_fox_logf, cache_moba_k, cache_moba_v, page_table,
           g_attn, w_in, b_forget, g_q_fox, g_k_fox, g_q_moba, g_k_moba, t5_table, w_branch_fox, w_branch_moba,
           w_out, g_ffn, w_router_group, w_router_expert, w_e_gate, w_e_up, w_e_down):
    depth = w_in.shape[0]
    assert depth == 1, "single-layer trunk"
    batch, seq, _ = x_prompt.shape
    nb, nt, _ = x_sample.shape
    assert nt == 1 and seq % (2 * MOBA_BLOCK) == 0
    n_p, n_s = batch * seq, nb * nt

    w = w_in[0]
    f0 = 3 * W_HEADS
    w_main = jnp.concatenate([w[:, :f0], w[:, f0 + N_HEADS:]], axis=1).astype(BF16)
    w_f = jnp.pad(w[:, f0:f0 + N_HEADS], ((0, 0), (0, LANES - N_HEADS))).astype(BF16)
    b_f = jnp.pad(b_forget[0], (0, LANES - N_HEADS))[None]
    gains = jnp.stack([jnp.tile(g[0], N_HEADS) for g in (g_q_fox, g_k_fox, g_q_moba, g_k_moba)])
    gains = jnp.pad(gains, ((0, 4), (0, 0)))
    seg = (jnp.arange(W_HEADS)[:, None] // HEAD_DIM == jnp.arange(W_HEADS)[None] // HEAD_DIM)
    seg = (seg.astype(F32) / HEAD_DIM).astype(BF16)
    wbf, wbm, wout = w_branch_fox[0].astype(BF16), w_branch_moba[0].astype(BF16), w_out[0].astype(BF16)
    w_router = jnp.concatenate([w_router_group[0], w_router_expert[0].reshape(D_MODEL, N_EXPERTS)], axis=1)
    w_router = jnp.pad(w_router, ((0, 0), (0, LANES - N_GROUPS - N_EXPERTS)))
    wg, wu, wd = w_e_gate[0].astype(BF16), w_e_up[0].astype(BF16), w_e_down[0].astype(BF16)
    proj_w = (g_attn, w_main, w_f, b_f, gains, seg)

    xp = x_prompt.reshape(n_p, D_MODEL)
    (q_f, kt_f, v_f, kt_f32, vt_f32, lf, q_m, q32_m, kt_m, v_m, kt_m32, vt_m32, kmean, sga, sgb) = \
        _project_prompt(xp, batch, seq, MOBA_BLOCK, *proj_w)
    cum = _cumsum(lf, batch, seq).reshape(batch, seq, LANES)
    rows3 = lambda a: a.reshape(batch, seq, W_HEADS)
    oa = _fox_prompt(q_f, cum, kt_f, _fox_key_rows(cum[:, :, :N_HEADS]), rows3(v_f))
    member, km, own_bias, prev_delta, prev_tile = _moba_side_operands(kmean, _moba_bias(t5_table), t5_table,
                                                                      batch, seq)
    ob = _moba_prompt(t5_table, q_m, q32_m, kt_m, member, rows3(v_m), km, own_bias, prev_delta, prev_tile)
    merge_w = (wbf, wbm, wout, g_ffn, w_router)
    x1, h2, route = _merge(xp, oa.reshape(n_p, W_HEADS), ob.reshape(n_p, W_HEADS), sga, sgb, *merge_w,
                           tm=MOBA_BLOCK)
    y_prompt = _moe(x1, h2, route, wg, wu, wd, tm=min(1024, n_p)).reshape(batch, seq, D_MODEL)

    xs = x_sample.reshape(n_s, D_MODEL)
    qa_s, ka_s, va_s, lf_s, qb_s, kb_s, vb_s, sga_s, sgb_s, _ = _project(xs, n_s, *proj_w)
    tokh = lambda a: a.reshape(n_s, N_HEADS, HEAD_DIM, 1)
    rows_last = lambda c: c[0].transpose(0, 2, 3, 1)
    tokrow = lambda a: a.reshape(n_s, 1, W_HEADS)
    oa_s = _fox_decode(page_table, tokrow(qa_s), tokrow(ka_s), tokrow(va_s), lf_s[:, :N_HEADS, None],
                       rows_last(cache_fox_k), rows_last(cache_fox_v), cache_fox_logf[0].transpose(0, 2, 1))
    ob_s = _moba_decode(page_table, tokh(qb_s), tokrow(qb_s), tokh(kb_s), tokh(vb_s), t5_table[:, :, None],
                        rows_last(cache_moba_k), rows_last(cache_moba_v))
    flat = lambda o: o.reshape(n_s, W_HEADS).astype(BF16)
    x1_s, h2_s, route_s = _merge(xs, flat(oa_s), flat(ob_s), sga_s, sgb_s, *merge_w, tm=n_s)
    y_sample = _moe(x1_s, h2_s, route_s, wg, wu, wd, tm=n_s).reshape(nb, nt, D_MODEL)

    kv_p = lambda a: a.transpose(0, 3, 1, 2)[None]
    kv_s = lambda a: a.reshape(depth, nb, nt, N_HEADS, HEAD_DIM)
    return (y_prompt, y_sample,
            kv_p(kt_f32), kv_p(vt_f32), lf[:, :N_HEADS].reshape(depth, batch, seq, N_HEADS),
            kv_p(kt_m32), kv_p(vt_m32),
            kv_s(ka_s), kv_s(va_s), lf_s[:, :N_HEADS].reshape(depth, nb, nt, N_HEADS), kv_s(kb_s), kv_s(vb_s))
```
